```python
import math
import jax, jax.numpy as jnp
from jax import lax
import numpy as np

D_MODEL = 2048
BATCH = 4
SEQ = 4096
DEPTH = 4
DEC_BATCH = 1
DEC_SEQ = 8192
PAST_LEN = 128

N_MEM = 256
MLA_HEADS = 8
QK_NOPE = 128
QK_ROPE = 64
V_DIM = 128
Q_LORA = 512
KV_LORA = 512
ROPE_THETA = 10000.0
Q_BLOCK = 128
WIN_HEADS = 8
WIN_KV_HEADS = 2
WIN_HD = 128
WINDOW = 128
BLOCK = 128
MEM_HEADS = 4
MEM_HD = 256
D_FF = 4 * D_MODEL
N_BRANCH = 3
EPS = 1e-6

MLA_WIDTH = MLA_HEADS * V_DIM
WIN_WIDTH = WIN_HEADS * WIN_HD
MEM_WIDTH = MEM_HEADS * MEM_HD
IN_SPLITS = (Q_LORA, KV_LORA, QK_ROPE, WIN_HEADS * WIN_HD, WIN_KV_HEADS * WIN_HD,
             WIN_KV_HEADS * WIN_HD, MEM_HEADS * MEM_HD, N_BRANCH * D_MODEL)
D_IN = sum(IN_SPLITS)

kernel_name = "hybrid_mla_window_memory_encoder"


def rmsnorm(x, g):
    xf = x.astype(jnp.float32)
    r = lax.rsqrt(jnp.mean(xf * xf, axis=-1, keepdims=True) + EPS)
    return (xf * r).astype(x.dtype) * g


def rope_tables(S):
    inv = 1.0 / (ROPE_THETA ** (jnp.arange(0, QK_ROPE, 2, dtype=jnp.float32) / QK_ROPE))
    ang = jnp.arange(S, dtype=jnp.float32)[:, None] * inv[None, :]
    return jnp.cos(ang), jnp.sin(ang)


def apply_rope(x, cos, sin):
    x1, x2 = jnp.split(x.astype(jnp.float32), 2, axis=-1)
    return jnp.concatenate([x1 * cos - x2 * sin, x1 * sin + x2 * cos], axis=-1).astype(x.dtype)


def alibi_slopes(n):
    return jnp.asarray(np.array([2.0 ** (-8.0 * (h + 1) / n) for h in range(n)]), dtype=jnp.float32)


def mla_attention(c_q, c_kv, k_rope, g_q, g_kv, w_uq, w_ukv):
    B, S, _ = c_q.shape
    H = MLA_HEADS
    q = (rmsnorm(c_q, g_q) @ w_uq).reshape(B, S, H, QK_NOPE + QK_ROPE)
    kv = (rmsnorm(c_kv, g_kv) @ w_ukv).reshape(B, S, H, QK_NOPE + V_DIM)
    cos, sin = rope_tables(S)
    q_nope = q[..., :QK_NOPE]
    q_rope = apply_rope(q[..., QK_NOPE:], cos[:, None, :], sin[:, None, :])
    k_nope, v = kv[..., :QK_NOPE], kv[..., QK_NOPE:]
    k_rope = apply_rope(k_rope, cos, sin)
    scale = (QK_NOPE + QK_ROPE) ** -0.5
    nb = S // Q_BLOCK
    qn_b = q_nope.reshape(B, nb, Q_BLOCK, H, QK_NOPE).transpose(1, 0, 2, 3, 4)
    qr_b = q_rope.reshape(B, nb, Q_BLOCK, H, QK_ROPE).transpose(1, 0, 2, 3, 4)

    def block(args):
        qn, qr = args
        s = jnp.einsum('bqhd,bkhd->bhqk', qn, k_nope) + jnp.einsum('bqhr,bkr->bhqk', qr, k_rope)
        p = jax.nn.softmax(s.astype(jnp.float32) * scale, axis=-1).astype(v.dtype)
        return jnp.einsum('bhqk,bkhd->bqhd', p, v)

    o = lax.map(block, (qn_b, qr_b))
    return o.transpose(1, 0, 2, 3, 4).reshape(B, S, MLA_WIDTH)


def window_attention(q, k, v, sink):
    B, S, _ = q.shape
    nb = S // BLOCK
    KV = WIN_KV_HEADS
    G = WIN_HEADS // WIN_KV_HEADS
    qb = q.reshape(B, nb, BLOCK, KV, G, WIN_HD)

    def band(t):
        t = t.reshape(B, S, KV, WIN_HD)
        tp = jnp.pad(t, ((0, 0), (BLOCK, BLOCK), (0, 0), (0, 0))).reshape(B, nb + 2, BLOCK, KV, WIN_HD)
        return jnp.concatenate([tp[:, :-2], tp[:, 1:-1], tp[:, 2:]], axis=2)

    kb, vb = band(k), band(v)
    s = jnp.einsum('bnqkgd,bnjkd->bnkgqj', qb, kb).astype(jnp.float32) * (WIN_HD ** -0.5)
    blk = jnp.arange(nb)[:, None, None] * BLOCK
    qpos = blk + jnp.arange(BLOCK)[None, :, None]
    kpos = blk - BLOCK + jnp.arange(3 * BLOCK)[None, None, :]
    dist = jnp.abs(qpos - kpos)
    valid = (dist <= WINDOW) & (kpos >= 0) & (kpos < S)
    slopes = alibi_slopes(WIN_HEADS).reshape(KV, G)
    bias = -slopes[None, :, :, None, None] * dist.astype(jnp.float32)[:, None, None]
    s = jnp.where(valid[:, None, None], s + bias, -jnp.inf)
    sink_l = sink.astype(jnp.float32).reshape(KV, G)[None, None, :, :, None, None]
    m = jnp.maximum(jnp.max(s, axis=-1, keepdims=True), sink_l)
    e = jnp.exp(s - m)
    p = e / (jnp.sum(e, axis=-1, keepdims=True) + jnp.exp(sink_l - m))
    o = jnp.einsum('bnkgqj,bnjkd->bnqkgd', p.astype(v.dtype), vb)
    return o.reshape(B, S, WIN_WIDTH)


def memory_attention(q, mem_kv):
    B, S, _ = q.shape
    M = mem_kv.shape[1]
    q = q.reshape(B, S, MEM_HEADS, MEM_HD)
    k, v = jnp.split(mem_kv, 2, axis=-1)
    k = k.reshape(B, M, MEM_HEADS, MEM_HD)
    v = v.reshape(B, M, MEM_HEADS, MEM_HD)
    s = jnp.einsum('bshd,bmhd->bhsm', q, k).astype(jnp.float32) * (MEM_HD ** -0.5)
    p = jax.nn.softmax(s, axis=-1).astype(v.dtype)
    return jnp.einsum('bhsm,bmhd->bshd', p, v).reshape(B, S, MEM_WIDTH)


def encoder_layer(x, mem, g_attn_pre, g_attn_post, w_in, g_q_lat, g_kv_lat, w_uq, w_ukv,
                  g_mem, w_mem_kv, sink_logit, w_branch_mla, w_branch_win, w_branch_mem, w_out,
                  g_ffn_pre, g_ffn_post, w_ff_up, w_ff_down):
    h = rmsnorm(x, g_attn_pre)
    z = h @ w_in
    split_points = np.cumsum(np.array(IN_SPLITS))[:-1].tolist()
    c_q, c_kv, k_rope, q_w, k_w, v_w, q_m, gates = jnp.split(z, split_points, axis=-1)
    a = mla_attention(c_q, c_kv, k_rope, g_q_lat, g_kv_lat, w_uq, w_ukv)
    b = window_attention(q_w, k_w, v_w, sink_logit)
    c = memory_attention(q_m, rmsnorm(mem, g_mem) @ w_mem_kv)
    g_a, g_b, g_c = jnp.split(jax.nn.sigmoid(gates), N_BRANCH, axis=-1)
    merged = g_a * (a @ w_branch_mla) + g_b * (b @ w_branch_win) + g_c * (c @ w_branch_mem)
    x = x + rmsnorm(merged @ w_out, g_attn_post)
    h = rmsnorm(x, g_ffn_pre)
    u = jnp.square(jax.nn.relu(h @ w_ff_up))
    return x + rmsnorm(u @ w_ff_down, g_ffn_post)


def setup_inputs(seed: int = 0) -> dict:
    key = jax.random.key(seed)
    ks = jax.random.split(key, 24)
    f32 = jnp.float32

    def w(k, shape, fan_in):
        return jax.random.normal(k, shape, f32) * (fan_in ** -0.5)

    def gain(k, shape):
        return 1.0 + 0.01 * jax.random.normal(k, shape, f32)

    L = DEPTH
    return {
        "x_prompt": jax.random.normal(ks[0], (BATCH, SEQ, D_MODEL), f32),
        "x_sample": jax.random.normal(ks[1], (DEC_BATCH, DEC_SEQ, D_MODEL), f32),
        "mem_prompt": jax.random.normal(ks[2], (BATCH, N_MEM, D_MODEL), f32),
        "mem_sample": jax.random.normal(ks[3], (DEC_BATCH, N_MEM, D_MODEL), f32),
        "g_attn_pre": gain(ks[4], (L, D_MODEL)),
        "g_attn_post": gain(ks[5], (L, D_MODEL)),
        "w_in": w(ks[6], (L, D_MODEL, D_IN), D_MODEL),
        "g_q_lat": gain(ks[7], (L, Q_LORA)),
        "g_kv_lat": gain(ks[8], (L, KV_LORA)),
        "w_uq": w(ks[9], (L, Q_LORA, MLA_HEADS * (QK_NOPE + QK_ROPE)), Q_LORA),
        "w_ukv": w(ks[10], (L, KV_LORA, MLA_HEADS * (QK_NOPE + V_DIM)), KV_LORA),
        "g_mem": gain(ks[11], (L, D_MODEL)),
        "w_mem_kv": w(ks[12], (L, D_MODEL, 2 * MEM_WIDTH), D_MODEL),
        "sink_logit": 0.5 * jax.random.normal(ks[13], (L, WIN_HEADS), f32),
        "w_branch_mla": w(ks[14], (L, MLA_WIDTH, D_MODEL), MLA_WIDTH),
        "w_branch_win": w(ks[15], (L, WIN_WIDTH, D_MODEL), WIN_WIDTH),
        "w_branch_mem": w(ks[16], (L, MEM_WIDTH, D_MODEL), MEM_WIDTH),
        "w_out": w(ks[17], (L, D_MODEL, D_MODEL), D_MODEL),
        "g_ffn_pre": gain(ks[18], (L, D_MODEL)),
        "g_ffn_post": gain(ks[19], (L, D_MODEL)),
        "w_ff_up": w(ks[20], (L, D_MODEL, D_FF), D_MODEL),
        "w_ff_down": w(ks[21], (L, D_FF, D_MODEL), D_FF),
    }


def reference(x_prompt, x_sample, mem_prompt, mem_sample,
              g_attn_pre, g_attn_post, w_in, g_q_lat, g_kv_lat, w_uq, w_ukv,
              g_mem, w_mem_kv, sink_logit, w_branch_mla, w_branch_win, w_branch_mem, w_out,
              g_ffn_pre, g_ffn_post, w_ff_up, w_ff_down):
    def run(x, mem):
        for l in range(DEPTH):
            x = encoder_layer(x, mem, g_attn_pre[l], g_attn_post[l], w_in[l], g_q_lat[l], g_kv_lat[l],
                              w_uq[l], w_ukv[l], g_mem[l], w_mem_kv[l], sink_logit[l],
                              w_branch_mla[l], w_branch_win[l], w_branch_mem[l], w_out[l],
                              g_ffn_pre[l], g_ffn_post[l], w_ff_up[l], w_ff_down[l])
        return x

    y_prompt = run(x_prompt, mem_prompt)
    y_sample = run(x_sample, mem_sample)
    return (y_prompt, y_sample)
```

```python
import functools

import jax
import jax.numpy as jnp
from jax import lax
from jax.experimental import pallas as pl
from jax.experimental.pallas import tpu as pltpu

F32 = jnp.float32
BF16 = jnp.bfloat16

EPS = 1e-6
ROPE_THETA = 10000.0

MLA_HEADS = 8
QK_NOPE = 128
QK_ROPE = 64
V_DIM = 128
Q_LORA = 512
KV_LORA = 512
WIN_HEADS = 8
WIN_KV_HEADS = 2
WIN_HD = 128
WINDOW = 128
BLOCK = 128
MEM_HEADS = 4
MEM_HD = 256
N_BRANCH = 3

LANES = 128
QK_PAD = 2 * LANES
LAT_W = Q_LORA + KV_LORA + LANES

VMEM_LIMIT = 56 * 1024 * 1024


def _cparams(sem):
    return pltpu.CompilerParams(dimension_semantics=sem, vmem_limit_bytes=VMEM_LIMIT)


def _rms(x, g):
    r = lax.rsqrt(jnp.mean(x * x, axis=-1, keepdims=True) + EPS)
    return (x * r) * g


def _tile(n, pref):
    t = min(n, pref)
    assert n % t == 0, (n, t)
    return t


def _norm_kernel(x_ref, g_ref, o_ref):
    o_ref[...] = _rms(x_ref[...], g_ref[...]).astype(o_ref.dtype)


def rmsnorm_bf16(x, g):
    T, D = x.shape
    tm = _tile(T, 512)
    return pl.pallas_call(
        _norm_kernel,
        grid=(T // tm,),
        in_specs=[pl.BlockSpec((tm, D), lambda i: (i, 0)),
                  pl.BlockSpec((1, D), lambda i: (0, 0))],
        out_specs=pl.BlockSpec((tm, D), lambda i: (i, 0)),
        out_shape=jax.ShapeDtypeStruct((T, D), BF16),
        compiler_params=_cparams(("parallel",)),
        name="rmsnorm",
    )(x, g.reshape(1, D))


def _mm_kernel(a_ref, w_ref, o_ref):
    o_ref[...] = jnp.dot(a_ref[...], w_ref[...], preferred_element_type=F32).astype(o_ref.dtype)


def matmul(a, w, out_dtype, tm=1024, tn=None, name="matmul"):
    T, K = a.shape
    N = w.shape[1]
    tm = _tile(T, tm)
    tn = N if tn is None else _tile(N, tn)
    return pl.pallas_call(
        _mm_kernel,
        grid=(T // tm, N // tn),
        in_specs=[pl.BlockSpec((tm, K), lambda i, j: (i, 0)),
                  pl.BlockSpec((K, tn), lambda i, j: (0, j))],
        out_specs=pl.BlockSpec((tm, tn), lambda i, j: (i, j)),
        out_shape=jax.ShapeDtypeStruct((T, N), out_dtype),
        compiler_params=_cparams(("parallel", "parallel")),
        name=name,
    )(a, w)


def _norm_mm_kernel(x_ref, g_ref, w_ref, o_ref):
    h = _rms(x_ref[...], g_ref[...]).astype(BF16)
    o_ref[...] = jnp.dot(h, w_ref[...], preferred_element_type=F32).astype(o_ref.dtype)


def norm_matmul(x, g, w, out_dtype, tm=256):
    T, K = x.shape
    N = w.shape[1]
    tm = _tile(T, tm)
    return pl.pallas_call(
        _norm_mm_kernel,
        grid=(T // tm,),
        in_specs=[pl.BlockSpec((tm, K), lambda i: (i, 0)),
                  pl.BlockSpec((1, K), lambda i: (0, 0)),
                  pl.BlockSpec((K, N), lambda i: (0, 0))],
        out_specs=pl.BlockSpec((tm, N), lambda i: (i, 0)),
        out_shape=jax.ShapeDtypeStruct((T, N), out_dtype),
        compiler_params=_cparams(("parallel",)),
        name="norm_matmul",
    )(x, g.reshape(1, K), w)


def _rope_tables(S, scale):
    half = QK_ROPE // 2
    inv = 1.0 / (ROPE_THETA ** (jnp.arange(0, QK_ROPE, 2, dtype=F32) / QK_ROPE))
    ang = jnp.arange(S, dtype=F32)[:, None] * inv[None, :]
    cos, sin = jnp.cos(ang) * scale, jnp.sin(ang) * scale
    z32 = jnp.zeros((S, half), F32)
    z64 = jnp.zeros((S, LANES - QK_ROPE), F32)
    cos_t = jnp.concatenate([cos, cos, z64], axis=1)
    sin_lo = jnp.concatenate([-sin, z32, z64], axis=1)
    sin_hi = jnp.concatenate([z32, sin, z64], axis=1)
    return cos_t, sin_lo, sin_hi


def _rope(v, cos_t, sin_lo, sin_hi):
    half = QK_ROPE // 2
    return (v * cos_t + pltpu.roll(v, half, axis=1) * sin_hi
            + pltpu.roll(v, LANES - half, axis=1) * sin_lo)


def _uq_kernel(scale, c_ref, g_ref, w_ref, cos_ref, slo_ref, shi_ref, o_ref):
    cn = _rms(c_ref[...], g_ref[...]).astype(BF16)
    q = jnp.dot(cn, w_ref[...], preferred_element_type=F32)
    cos_t, sin_lo, sin_hi = cos_ref[...], slo_ref[...], shi_ref[...]
    for h in range(MLA_HEADS):
        base = h * QK_PAD
        o_ref[:, base:base + LANES] = (q[:, base:base + LANES] * scale).astype(o_ref.dtype)
        r = q[:, base + LANES:base + QK_PAD]
        o_ref[:, base + LANES:base + QK_PAD] = _rope(r, cos_t, sin_lo, sin_hi).astype(o_ref.dtype)


def mla_q_proj(z_lat, g_q, w_qm, S, scale, tm=512):
    T = z_lat.shape[0]
    tm = _tile(S, tm)
    nS = S // tm
    cos_t, sin_lo, sin_hi = _rope_tables(S, scale)
    tab = pl.BlockSpec((tm, LANES), lambda i: (i % nS, 0))
    N = MLA_HEADS * QK_PAD
    return pl.pallas_call(
        functools.partial(_uq_kernel, scale),
        grid=(T // tm,),
        in_specs=[pl.BlockSpec((tm, Q_LORA), lambda i: (i, 0)),
                  pl.BlockSpec((1, Q_LORA), lambda i: (0, 0)),
                  pl.BlockSpec((Q_LORA, N), lambda i: (0, 0)),
                  tab, tab, tab],
        out_specs=pl.BlockSpec((tm, N), lambda i: (i, 0)),
        out_shape=jax.ShapeDtypeStruct((T, N), BF16),
        compiler_params=_cparams(("parallel",)),
        name="mla_q_proj",
    )(z_lat, g_q.reshape(1, Q_LORA), w_qm, cos_t, sin_lo, sin_hi)


def _ukv_kernel(c_ref, kr_ref, g_ref, w_ref, cos_ref, slo_ref, shi_ref, k_ref, v_ref):
    cn = _rms(c_ref[...], g_ref[...]).astype(BF16)
    kv = jnp.dot(cn, w_ref[...], preferred_element_type=F32)
    kr = _rope(kr_ref[...], cos_ref[...], slo_ref[...], shi_ref[...]).astype(k_ref.dtype)
    nk = MLA_HEADS * QK_NOPE
    for h in range(MLA_HEADS):
        base = h * QK_PAD
        k_ref[:, base:base + LANES] = kv[:, h * QK_NOPE:(h + 1) * QK_NOPE].astype(k_ref.dtype)
        k_ref[:, base + LANES:base + QK_PAD] = kr
    v_ref[...] = kv[:, nk:].astype(v_ref.dtype)


def mla_kv_proj(z_lat, g_kv, w_kv, S, tm=512):
    T = z_lat.shape[0]
    tm = _tile(S, tm)
    nS = S // tm
    cos_t, sin_lo, sin_hi = _rope_tables(S, 1.0)
    tab = pl.BlockSpec((tm, LANES), lambda i: (i % nS, 0))
    NK = MLA_HEADS * QK_PAD
    NV = MLA_HEADS * V_DIM
    kr_blk = (Q_LORA + KV_LORA) // LANES
    return pl.pallas_call(
        _ukv_kernel,
        grid=(T // tm,),
        in_specs=[pl.BlockSpec((tm, KV_LORA), lambda i: (i, 1)),
                  pl.BlockSpec((tm, LANES), lambda i: (i, kr_blk)),
                  pl.BlockSpec((1, KV_LORA), lambda i: (0, 0)),
                  pl.BlockSpec((KV_LORA, MLA_HEADS * (QK_NOPE + V_DIM)), lambda i: (0, 0)),
                  tab, tab, tab],
        out_specs=[pl.BlockSpec((tm, NK), lambda i: (i, 0)),
                   pl.BlockSpec((tm, NV), lambda i: (i, 0))],
        out_shape=[jax.ShapeDtypeStruct((T, NK), BF16),
                   jax.ShapeDtypeStruct((T, NV), BF16)],
        compiler_params=_cparams(("parallel",)),
        name="mla_kv_proj",
    )(z_lat, z_lat, g_kv.reshape(1, KV_LORA), w_kv, cos_t, sin_lo, sin_hi)


def _mla_kernel(tq, tk, q_ref, k_ref, v_ref, o_ref):
    S = q_ref.shape[0]
    nq, nk = S // tq, S // tk

    def q_body(qi, _):
        q0 = pl.multiple_of(qi * tq, tq)
        q = q_ref[pl.ds(q0, tq), :]

        def k_body(ki, carry):
            m, l, acc = carry
            k0 = pl.multiple_of(ki * tk, tk)
            k = k_ref[pl.ds(k0, tk), :]
            v = v_ref[pl.ds(k0, tk), :]
            s = lax.dot_general(q, k, (((1,), (1,)), ((), ())), preferred_element_type=F32)
            m_new = jnp.maximum(m, jnp.max(s, axis=1, keepdims=True))
            alpha = jnp.exp(m - m_new)
            p = jnp.exp(s - m_new)
            l = alpha * l + jnp.sum(p, axis=1, keepdims=True)
            acc = alpha * acc + jnp.dot(p.astype(BF16), v, preferred_element_type=F32)
            return m_new, l, acc

        m0 = jnp.full((tq, 1), -jnp.inf, F32)
        l0 = jnp.zeros((tq, 1), F32)
        a0 = jnp.zeros((tq, V_DIM), F32)
        m, l, acc = lax.fori_loop(0, nk, k_body, (m0, l0, a0))
        o_ref[pl.ds(q0, tq), :] = (acc / l).astype(o_ref.dtype)
        return 0

    lax.fori_loop(0, nq, q_body, 0)


def mla_attention(q, k, v, B, S, tq=256, tk=512):
    T = B * S
    tq, tk = _tile(S, tq), _tile(S, tk)
    return pl.pallas_call(
        functools.partial(_mla_kernel, tq, tk),
        grid=(B, MLA_HEADS),
        in_specs=[pl.BlockSpec((S, QK_PAD), lambda b, h: (b, h)),
                  pl.BlockSpec((S, QK_PAD), lambda b, h: (b, h)),
                  pl.BlockSpec((S, V_DIM), lambda b, h: (b, h))],
        out_specs=pl.BlockSpec((S, V_DIM), lambda b, h: (b, h)),
        out_shape=jax.ShapeDtypeStruct((T, MLA_HEADS * V_DIM), BF16),
        compiler_params=_cparams(("parallel", "parallel")),
        name="mla_attention",
    )(q, k, v)


def _alibi_slope(h):
    return 2.0 ** (-8.0 * (h + 1) / WIN_HEADS)


def _win_kernel(S, tq, sink_ref, q_ref, kp_ref, kc_ref, kn_ref, vp_ref, vc_ref, vn_ref, o_ref):
    i = pl.program_id(1)
    G = WIN_HEADS // WIN_KV_HEADS
    scale = WIN_HD ** -0.5
    kb = jnp.concatenate([kp_ref[...], kc_ref[...], kn_ref[...]], axis=0)
    vb = jnp.concatenate([vp_ref[...], vc_ref[...], vn_ref[...]], axis=0)
    rows = lax.broadcasted_iota(jnp.int32, (G * BLOCK, 3 * BLOCK), 0)
    cols = lax.broadcasted_iota(jnp.int32, (G * BLOCK, 3 * BLOCK), 1)
    qoff = rows % BLOCK
    dist = jnp.abs(qoff + BLOCK - cols)
    distf = dist.astype(F32)
    gidx = rows // BLOCK
    for j in range(tq // BLOCK):
        kstart = i * tq + (j - 1) * BLOCK
        kpos = kstart + cols
        valid = (dist <= WINDOW) & (kpos >= 0) & (kpos < S)
        for kvh in range(WIN_KV_HEADS):
            qs = jnp.concatenate(
                [q_ref[j * BLOCK:(j + 1) * BLOCK, (kvh * G + g) * WIN_HD:(kvh * G + g + 1) * WIN_HD]
                 for g in range(G)], axis=0)
            kk = kb[j * BLOCK:(j + 3) * BLOCK, kvh * WIN_HD:(kvh + 1) * WIN_HD]
            vv = vb[j * BLOCK:(j + 3) * BLOCK, kvh * WIN_HD:(kvh + 1) * WIN_HD]
            s = lax.dot_general(qs, kk, (((1,), (1,)), ((), ())), preferred_element_type=F32) * scale
            slope = jnp.zeros_like(distf)
            sink = jnp.zeros((G * BLOCK, 1), F32)
            for g in range(G):
                hsel = gidx == g
                slope = jnp.where(hsel, _alibi_slope(kvh * G + g), slope)
                sink = jnp.where(hsel[:, :1], sink_ref[kvh * G + g], sink)
            s = jnp.where(valid, s - slope * distf, -jnp.inf)
            m = jnp.maximum(jnp.max(s, axis=1, keepdims=True), sink)
            e = jnp.exp(s - m)
            p = e / (jnp.sum(e, axis=1, keepdims=True) + jnp.exp(sink - m))
            o = jnp.dot(p.astype(BF16), vv, preferred_element_type=F32)
            for g in range(G):
                c0 = (kvh * G + g) * WIN_HD
                o_ref[j * BLOCK:(j + 1) * BLOCK, c0:c0 + WIN_HD] = (
                    o[g * BLOCK:(g + 1) * BLOCK].astype(o_ref.dtype))


def window_attention(zw, sink, B, S, tq=512):
    T = B * S
    tq = _tile(S, tq)
    nq = S // tq
    r = tq // BLOCK
    nb = S // BLOCK
    KW = WIN_KV_HEADS * WIN_HD
    kcol = (WIN_HEADS * WIN_HD + MEM_HEADS * MEM_HD) // KW
    vcol = kcol + 1

    def prev_idx(b, i, col):
        return (b * nb + jnp.maximum(i * r - 1, 0), col)

    def next_idx(b, i, col):
        return (b * nb + jnp.minimum((i + 1) * r, nb - 1), col)

    in_specs = [
        pl.BlockSpec(memory_space=pltpu.SMEM),
        pl.BlockSpec((tq, WIN_HEADS * WIN_HD), lambda b, i: (b * nq + i, 0)),
        pl.BlockSpec((BLOCK, KW), lambda b, i: prev_idx(b, i, kcol)),
        pl.BlockSpec((tq, KW), lambda b, i: (b * nq + i, kcol)),
        pl.BlockSpec((BLOCK, KW), lambda b, i: next_idx(b, i, kcol)),
        pl.BlockSpec((BLOCK, KW), lambda b, i: prev_idx(b, i, vcol)),
        pl.BlockSpec((tq, KW), lambda b, i: (b * nq + i, vcol)),
        pl.BlockSpec((BLOCK, KW), lambda b, i: next_idx(b, i, vcol)),
    ]
    return pl.pallas_call(
        functools.partial(_win_kernel, S, tq),
        grid=(B, nq),
        in_specs=in_specs,
        out_specs=pl.BlockSpec((tq, WIN_HEADS * WIN_HD), lambda b, i: (b * nq + i, 0)),
        out_shape=jax.ShapeDtypeStruct((T, WIN_HEADS * WIN_HD), BF16),
        compiler_params=_cparams(("parallel", "parallel")),
        name="window_attention",
    )(sink, zw, zw, zw, zw, zw, zw, zw)


def _mem_kernel(q_ref, kv_ref, o_ref):
    scale = MEM_HD ** -0.5
    W = MEM_HEADS * MEM_HD
    for h in range(MEM_HEADS):
        q = q_ref[:, h * MEM_HD:(h + 1) * MEM_HD]
        k = kv_ref[:, h * MEM_HD:(h + 1) * MEM_HD]
        v = kv_ref[:, W + h * MEM_HD:W + (h + 1) * MEM_HD]
        s = lax.dot_general(q, k, (((1,), (1,)), ((), ())), preferred_element_type=F32) * scale
        m = jnp.max(s, axis=1, keepdims=True)
        e = jnp.exp(s - m)
        p = e / jnp.sum(e, axis=1, keepdims=True)
        o_ref[:, h * MEM_HD:(h + 1) * MEM_HD] = jnp.dot(
            p.astype(BF16), v, preferred_element_type=F32).astype(o_ref.dtype)


def memory_attention(zw, mem_kv, B, S, tq=512):
    T = B * S
    tq = _tile(S, tq)
    nq = S // tq
    M = mem_kv.shape[0] // B
    W = MEM_HEADS * MEM_HD
    return pl.pallas_call(
        _mem_kernel,
        grid=(B, nq),
        in_specs=[pl.BlockSpec((tq, W), lambda b, i: (b * nq + i, 1)),
                  pl.BlockSpec((M, 2 * W), lambda b, i: (b, 0))],
        out_specs=pl.BlockSpec((tq, W), lambda b, i: (b * nq + i, 0)),
        out_shape=jax.ShapeDtypeStruct((T, W), BF16),
        compiler_params=_cparams(("parallel", "parallel")),
        name="memory_attention",
    )(zw, mem_kv)


def _merge_kernel(h_ref, a_ref, b_ref, c_ref, wga_ref, wgb_ref, wgc_ref,
                  wa_ref, wb_ref, wc_ref, o_ref):
    h = h_ref[...]

    def branch(x_ref, wg_ref, w_ref):
        gate = jax.nn.sigmoid(jnp.dot(h, wg_ref[...], preferred_element_type=F32))
        return gate * jnp.dot(x_ref[...], w_ref[...], preferred_element_type=F32)

    o_ref[...] = (branch(a_ref, wga_ref, wa_ref) + branch(b_ref, wgb_ref, wb_ref)
                  + branch(c_ref, wgc_ref, wc_ref)).astype(o_ref.dtype)


def gated_merge(h, a, b, c, w_g, w_a, w_b, w_c, tm=1024, tn=256):
    T, D = h.shape
    tm, tn = _tile(T, tm), _tile(D, tn)
    nj = D // tn
    act = lambda w: pl.BlockSpec((tm, w), lambda i, j: (i, 0))
    wg = lambda br: pl.BlockSpec((D, tn), lambda i, j: (0, br * nj + j))
    wx = lambda k: pl.BlockSpec((k, tn), lambda i, j: (0, j))
    return pl.pallas_call(
        _merge_kernel,
        grid=(T // tm, nj),
        in_specs=[act(D), act(a.shape[1]), act(b.shape[1]), act(c.shape[1]),
                  wg(0), wg(1), wg(2),
                  wx(a.shape[1]), wx(b.shape[1]), wx(c.shape[1])],
        out_specs=pl.BlockSpec((tm, tn), lambda i, j: (i, j)),
        out_shape=jax.ShapeDtypeStruct((T, D), BF16),
        compiler_params=_cparams(("parallel", "parallel")),
        name="gated_merge",
    )(h, a, b, c, w_g, w_g, w_g, w_a, w_b, w_c)


def _outproj_kernel(m_ref, w_ref, x_ref, gpost_ref, gnext_ref, xo_ref, ho_ref):
    y = jnp.dot(m_ref[...], w_ref[...], preferred_element_type=F32)
    x_new = x_ref[...] + _rms(y, gpost_ref[...])
    xo_ref[...] = x_new
    ho_ref[...] = _rms(x_new, gnext_ref[...]).astype(ho_ref.dtype)


def out_proj(merged, w_out, x, g_post, g_next, tm=512):
    T, D = x.shape
    tm = _tile(T, tm)
    row = lambda: pl.BlockSpec((tm, D), lambda i: (i, 0))
    vec = lambda: pl.BlockSpec((1, D), lambda i: (0, 0))
    return pl.pallas_call(
        _outproj_kernel,
        grid=(T // tm,),
        in_specs=[row(), pl.BlockSpec((D, D), lambda i: (0, 0)), row(), vec(), vec()],
        out_specs=[row(), row()],
        out_shape=[jax.ShapeDtypeStruct((T, D), F32), jax.ShapeDtypeStruct((T, D), BF16)],
        compiler_params=_cparams(("parallel",)),
        name="out_proj",
    )(merged, w_out, x, g_post.reshape(1, D), g_next.reshape(1, D))


def _ffn_kernel(h_ref, wu_ref, wd_ref, x_ref, gpost_ref, gnext_ref, xo_ref, ho_ref, acc_ref):
    k = pl.program_id(1)

    @pl.when(k == 0)
    def _():
        acc_ref[...] = jnp.zeros_like(acc_ref)

    u = jnp.dot(h_ref[...], wu_ref[...], preferred_element_type=F32)
    u = jnp.square(jnp.maximum(u, 0.0)).astype(BF16)
    acc_ref[...] += jnp.dot(u, wd_ref[...], preferred_element_type=F32)

    @pl.when(k == pl.num_programs(1) - 1)
    def _():
        x_new = x_ref[...] + _rms(acc_ref[...], gpost_ref[...])
        xo_ref[...] = x_new
        ho_ref[...] = _rms(x_new, gnext_ref[...]).astype(ho_ref.dtype)


def ffn(h, w_up, w_down, x, g_post, g_next, tm=512, tf=512):
    T, D = x.shape
    Fd = w_up.shape[1]
    tm, tf = _tile(T, tm), _tile(Fd, tf)
    row = lambda: pl.BlockSpec((tm, D), lambda i, k: (i, 0))
    vec = lambda: pl.BlockSpec((1, D), lambda i, k: (0, 0))
    return pl.pallas_call(
        _ffn_kernel,
        grid=(T // tm, Fd // tf),
        in_specs=[row(),
                  pl.BlockSpec((D, tf), lambda i, k: (0, k)),
                  pl.BlockSpec((tf, D), lambda i, k: (k, 0)),
                  row(), vec(), vec()],
        out_specs=[row(), row()],
        out_shape=[jax.ShapeDtypeStruct((T, D), F32), jax.ShapeDtypeStruct((T, D), BF16)],
        scratch_shapes=[pltpu.VMEM((tm, D), F32)],
        compiler_params=_cparams(("parallel", "arbitrary")),
        name="ffn",
    )(h, w_up, w_down, x, g_post.reshape(1, D), g_next.reshape(1, D))


def _prep_layer(w_in, w_uq, w_ukv):
    D = w_in.shape[0]
    o_lat = Q_LORA + KV_LORA + QK_ROPE
    o_qw = o_lat + WIN_HEADS * WIN_HD
    o_kw = o_qw + WIN_KV_HEADS * WIN_HD
    o_vw = o_kw + WIN_KV_HEADS * WIN_HD
    o_qm = o_vw + MEM_HEADS * MEM_HD
    w_lat = jnp.concatenate([w_in[:, :o_lat], jnp.zeros((D, LAT_W - o_lat), w_in.dtype)], axis=1)
    w_zw = jnp.concatenate([w_in[:, o_lat:o_qw], w_in[:, o_vw:o_qm],
                            w_in[:, o_qw:o_kw], w_in[:, o_kw:o_vw]], axis=1)
    w_g = w_in[:, o_qm:]
    wq = w_uq.reshape(Q_LORA, MLA_HEADS, QK_NOPE + QK_ROPE)
    wq = jnp.pad(wq, ((0, 0), (0, 0), (0, QK_PAD - QK_NOPE - QK_ROPE))).reshape(Q_LORA, MLA_HEADS * QK_PAD)
    wkv = w_ukv.reshape(KV_LORA, MLA_HEADS, QK_NOPE + V_DIM)
    wkv = jnp.concatenate([wkv[:, :, :QK_NOPE].reshape(KV_LORA, -1),
                           wkv[:, :, QK_NOPE:].reshape(KV_LORA, -1)], axis=1)
    return (w_lat.astype(BF16), w_zw.astype(BF16), w_g.astype(BF16),
            wq.astype(BF16), wkv.astype(BF16))


def kernel(x_prompt, x_sample, mem_prompt, mem_sample, g_attn_pre, g_attn_post, w_in, g_q_lat, g_kv_lat, w_uq, w_ukv, g_mem, w_mem_kv, sink_logit, w_branch_mla, w_branch_win, w_branch_mem, w_out, g_ffn_pre, g_ffn_post, w_ff_up, w_ff_down):
    depth = w_in.shape[0]
    scale = (QK_NOPE + QK_ROPE) ** -0.5
    layers = []
    for l in range(depth):
        layers.append(_prep_layer(w_in[l], w_uq[l], w_ukv[l]) + (
            w_mem_kv[l].astype(BF16), w_branch_mla[l].astype(BF16), w_branch_win[l].astype(BF16),
            w_branch_mem[l].astype(BF16), w_out[l].astype(BF16),
            w_ff_up[l].astype(BF16), w_ff_down[l].astype(BF16)))

    def run(x3, mem3):
        B, S, D = x3.shape
        x = x3.reshape(B * S, D)
        mem = mem3.reshape(B * mem3.shape[1], D)
        h = rmsnorm_bf16(x, g_attn_pre[0])
        for l in range(depth):
            (w_lat, w_zw, w_g, wq, wkv, wmem, wa, wb, wc, wo, wup, wdn) = layers[l]
            z_lat = matmul(h, w_lat, F32, name="in_proj_latent")
            zw = matmul(h, w_zw, BF16, tn=w_zw.shape[1] // 2, name="in_proj_qkv")
            q = mla_q_proj(z_lat, g_q_lat[l], wq, S, scale)
            k, v = mla_kv_proj(z_lat, g_kv_lat[l], wkv, S)
            a = mla_attention(q, k, v, B, S)
            b = window_attention(zw, sink_logit[l], B, S)
            mem_kv = norm_matmul(mem, g_mem[l], wmem, BF16)
            c = memory_attention(zw, mem_kv, B, S)
            merged = gated_merge(h, a, b, c, w_g, wa, wb, wc)
            x, h = out_proj(merged, wo, x, g_attn_post[l], g_ffn_pre[l])
            g_next = g_attn_pre[l + 1] if l + 1 < depth else g_attn_pre[l]
            x, h = ffn(h, wup, wdn, x, g_ffn_post[l], g_next)
        return x.reshape(B, S, D)

    return (run(x_prompt, mem_prompt), run(x_sample, mem_sample))
```

```python
import functools
import math

import jax
import jax.numpy as jnp
from jax import lax
from jax.experimental import pallas as pl
from jax.experimental.pallas import tpu as pltpu

F32 = jnp.float32
BF16 = jnp.bfloat16

EPS = 1e-6
ROPE_THETA = 10000.0

MLA_HEADS = 8
QK_NOPE = 128
QK_ROPE = 64
V_DIM = 128
Q_LORA = 512
KV_LORA = 512
WIN_HEADS = 8
WIN_KV_HEADS = 2
WIN_HD = 128
WINDOW = 128
BLOCK = 128
MEM_HEADS = 4
MEM_HD = 256
N_BRANCH = 3

LANES = 128
QK_PAD = 2 * LANES
LAT_W = Q_LORA + KV_LORA + LANES

VMEM_LIMIT = 56 * 1024 * 1024


def _cparams(sem):
    return pltpu.CompilerParams(dimension_semantics=sem, vmem_limit_bytes=VMEM_LIMIT)


def _rms(x, g):
    r = lax.rsqrt(jnp.mean(x * x, axis=-1, keepdims=True) + EPS)
    return (x * r) * g


def _tile(n, pref):
    t = min(n, pref)
    assert n % t == 0, (n, t)
    return t


def _norm_kernel(x_ref, g_ref, o_ref):
    o_ref[...] = _rms(x_ref[...], g_ref[...]).astype(o_ref.dtype)


def rmsnorm_bf16(x, g):
    T, D = x.shape
    tm = _tile(T, 512)
    return pl.pallas_call(
        _norm_kernel,
        grid=(T // tm,),
        in_specs=[pl.BlockSpec((tm, D), lambda i: (i, 0)),
                  pl.BlockSpec((1, D), lambda i: (0, 0))],
        out_specs=pl.BlockSpec((tm, D), lambda i: (i, 0)),
        out_shape=jax.ShapeDtypeStruct((T, D), BF16),
        compiler_params=_cparams(("parallel",)),
        name="rmsnorm",
    )(x, g.reshape(1, D))


def _mm_kernel(a_ref, w_ref, o_ref):
    o_ref[...] = jnp.dot(a_ref[...], w_ref[...], preferred_element_type=F32).astype(o_ref.dtype)


def matmul(a, w, out_dtype, tm=1024, tn=None, name="matmul"):
    T, K = a.shape
    N = w.shape[1]
    tm = _tile(T, tm)
    tn = N if tn is None else _tile(N, tn)
    return pl.pallas_call(
        _mm_kernel,
        grid=(T // tm, N // tn),
        in_specs=[pl.BlockSpec((tm, K), lambda i, j: (i, 0)),
                  pl.BlockSpec((K, tn), lambda i, j: (0, j))],
        out_specs=pl.BlockSpec((tm, tn), lambda i, j: (i, j)),
        out_shape=jax.ShapeDtypeStruct((T, N), out_dtype),
        compiler_params=_cparams(("parallel", "parallel")),
        name=name,
    )(a, w)


def _norm_mm_kernel(x_ref, g_ref, w_ref, o_ref):
    h = _rms(x_ref[...], g_ref[...]).astype(BF16)
    o_ref[...] = jnp.dot(h, w_ref[...], preferred_element_type=F32).astype(o_ref.dtype)


def norm_matmul(x, g, w, out_dtype, tm=256):
    T, K = x.shape
    N = w.shape[1]
    tm = _tile(T, tm)
    return pl.pallas_call(
        _norm_mm_kernel,
        grid=(T // tm,),
        in_specs=[pl.BlockSpec((tm, K), lambda i: (i, 0)),
                  pl.BlockSpec((1, K), lambda i: (0, 0)),
                  pl.BlockSpec((K, N), lambda i: (0, 0))],
        out_specs=pl.BlockSpec((tm, N), lambda i: (i, 0)),
        out_shape=jax.ShapeDtypeStruct((T, N), out_dtype),
        compiler_params=_cparams(("parallel",)),
        name="norm_matmul",
    )(x, g.reshape(1, K), w)


def _rope_tables(S, scale):
    half = QK_ROPE // 2
    inv = 1.0 / (ROPE_THETA ** (jnp.arange(0, QK_ROPE, 2, dtype=F32) / QK_ROPE))
    ang = jnp.arange(S, dtype=F32)[:, None] * inv[None, :]
    cos, sin = jnp.cos(ang) * scale, jnp.sin(ang) * scale
    z32 = jnp.zeros((S, half), F32)
    z64 = jnp.zeros((S, LANES - QK_ROPE), F32)
    cos_t = jnp.concatenate([cos, cos, z64], axis=1)
    sin_lo = jnp.concatenate([-sin, z32, z64], axis=1)
    sin_hi = jnp.concatenate([z32, sin, z64], axis=1)
    return cos_t, sin_lo, sin_hi


def _rope(v, cos_t, sin_lo, sin_hi):
    half = QK_ROPE // 2
    return (v * cos_t + pltpu.roll(v, half, axis=1) * sin_hi
            + pltpu.roll(v, LANES - half, axis=1) * sin_lo)


def _uq_kernel(scale, c_ref, g_ref, w_ref, cos_ref, slo_ref, shi_ref, o_ref):
    cn = _rms(c_ref[...], g_ref[...]).astype(BF16)
    q = jnp.dot(cn, w_ref[...], preferred_element_type=F32)
    cos_t, sin_lo, sin_hi = cos_ref[...], slo_ref[...], shi_ref[...]
    for h in range(MLA_HEADS):
        base = h * QK_PAD
        o_ref[:, base:base + LANES] = (q[:, base:base + LANES] * scale).astype(o_ref.dtype)
        r = q[:, base + LANES:base + QK_PAD]
        o_ref[:, base + LANES:base + QK_PAD] = _rope(r, cos_t, sin_lo, sin_hi).astype(o_ref.dtype)


def mla_q_proj(z_lat, g_q, w_qm, S, scale, tm=512):
    T = z_lat.shape[0]
    tm = _tile(S, tm)
    nS = S // tm
    cos_t, sin_lo, sin_hi = _rope_tables(S, scale)
    tab = pl.BlockSpec((tm, LANES), lambda i: (i % nS, 0))
    N = MLA_HEADS * QK_PAD
    return pl.pallas_call(
        functools.partial(_uq_kernel, scale),
        grid=(T // tm,),
        in_specs=[pl.BlockSpec((tm, Q_LORA), lambda i: (i, 0)),
                  pl.BlockSpec((1, Q_LORA), lambda i: (0, 0)),
                  pl.BlockSpec((Q_LORA, N), lambda i: (0, 0)),
                  tab, tab, tab],
        out_specs=pl.BlockSpec((tm, N), lambda i: (i, 0)),
        out_shape=jax.ShapeDtypeStruct((T, N), BF16),
        compiler_params=_cparams(("parallel",)),
        name="mla_q_proj",
    )(z_lat, g_q.reshape(1, Q_LORA), w_qm, cos_t, sin_lo, sin_hi)


def _ukv_kernel(c_ref, kr_ref, g_ref, w_ref, cos_ref, slo_ref, shi_ref, k_ref, v_ref):
    cn = _rms(c_ref[...], g_ref[...]).astype(BF16)
    kv = jnp.dot(cn, w_ref[...], preferred_element_type=F32)
    kr = _rope(kr_ref[...], cos_ref[...], slo_ref[...], shi_ref[...]).astype(k_ref.dtype)
    nk = MLA_HEADS * QK_NOPE
    lane = lax.broadcasted_iota(jnp.int32, (kv.shape[0], LANES), 1)
    ones_col = jnp.where(lane == 0, 1.0, 0.0).astype(v_ref.dtype)
    for h in range(MLA_HEADS):
        base = h * QK_PAD
        k_ref[:, base:base + LANES] = kv[:, h * QK_NOPE:(h + 1) * QK_NOPE].astype(k_ref.dtype)
        k_ref[:, base + LANES:base + QK_PAD] = kr
        v_ref[:, base:base + LANES] = kv[:, nk + h * V_DIM:nk + (h + 1) * V_DIM].astype(v_ref.dtype)
        v_ref[:, base + LANES:base + QK_PAD] = ones_col


def mla_kv_proj(z_lat, g_kv, w_kv, S, tm=512):
    T = z_lat.shape[0]
    tm = _tile(S, tm)
    nS = S // tm
    cos_t, sin_lo, sin_hi = _rope_tables(S, 1.0)
    tab = pl.BlockSpec((tm, LANES), lambda i: (i % nS, 0))
    NK = MLA_HEADS * QK_PAD
    NV = MLA_HEADS * QK_PAD
    kr_blk = (Q_LORA + KV_LORA) // LANES
    return pl.pallas_call(
        _ukv_kernel,
        grid=(T // tm,),
        in_specs=[pl.BlockSpec((tm, KV_LORA), lambda i: (i, 1)),
                  pl.BlockSpec((tm, LANES), lambda i: (i, kr_blk)),
                  pl.BlockSpec((1, KV_LORA), lambda i: (0, 0)),
                  pl.BlockSpec((KV_LORA, MLA_HEADS * (QK_NOPE + V_DIM)), lambda i: (0, 0)),
                  tab, tab, tab],
        out_specs=[pl.BlockSpec((tm, NK), lambda i: (i, 0)),
                   pl.BlockSpec((tm, NV), lambda i: (i, 0))],
        out_shape=[jax.ShapeDtypeStruct((T, NK), BF16),
                   jax.ShapeDtypeStruct((T, NV), BF16)],
        compiler_params=_cparams(("parallel",)),
        name="mla_kv_proj",
    )(z_lat, z_lat, g_kv.reshape(1, KV_LORA), w_kv, cos_t, sin_lo, sin_hi)


def _mla_kernel(tq, tk, nchain, q_ref, k_ref, v_ref, o_ref, sa_ref, sb_ref):
    S = q_ref.shape[0]
    nq, nk = S // (tq * nchain), S // tk
    nsteps = nq * nk
    assert nsteps % 2 == 0

    def rows(qi, c):
        return pl.ds(pl.multiple_of((qi * nchain + c) * tq, tq), tq)

    def scores(t, dst_ref):
        qi = t // nk
        ki = t - qi * nk
        k = k_ref[pl.ds(pl.multiple_of(ki * tk, tk), tk), :]
        for c in range(nchain):
            dst_ref[c] = lax.dot_general(q_ref[rows(qi, c), :], k, (((1,), (1,)), ((), ())),
                                         preferred_element_type=F32)

    def step(t, cur_ref, nxt_ref, carry):
        t_next = jnp.where(t + 1 == nsteps, 0, t + 1)
        scores(t_next, nxt_ref)
        qi = t // nk
        ki = t - qi * nk
        v = v_ref[pl.ds(pl.multiple_of(ki * tk, tk), tk), :]
        out = []
        for c in range(nchain):
            m, acc = carry[c]
            m = jnp.where(ki == 0, -jnp.inf, m)
            s = cur_ref[c]
            m_new = jnp.maximum(m, jnp.max(s, axis=1, keepdims=True))
            alpha = jnp.exp2(m - m_new)
            p = jnp.exp2(s - m_new).astype(BF16)
            acc = alpha * acc + jnp.dot(p, v, preferred_element_type=F32)
            o_ref[rows(qi, c), :] = (acc[:, :V_DIM] / acc[:, V_DIM:V_DIM + 1]).astype(o_ref.dtype)
            out.append((m_new, acc))
        return tuple(out)

    scores(0, sa_ref)

    def body(i, carry):
        carry = step(2 * i, sa_ref, sb_ref, carry)
        return step(2 * i + 1, sb_ref, sa_ref, carry)

    init = tuple((jnp.full((tq, 1), -jnp.inf, F32), jnp.zeros((tq, QK_PAD), F32))
                 for _ in range(nchain))
    lax.fori_loop(0, nsteps // 2, body, init)


def mla_attention(q, k, v, B, S, tq=256, tk=2048, nchain=2):
    T = B * S
    tq, tk = _tile(S, tq), _tile(S, tk)
    assert S % (tq * nchain) == 0
    return pl.pallas_call(
        functools.partial(_mla_kernel, tq, tk, nchain),
        grid=(B, MLA_HEADS),
        in_specs=[pl.BlockSpec((S, QK_PAD), lambda b, h: (b, h)),
                  pl.BlockSpec((S, QK_PAD), lambda b, h: (b, h)),
                  pl.BlockSpec((S, QK_PAD), lambda b, h: (b, h))],
        out_specs=pl.BlockSpec((S, V_DIM), lambda b, h: (b, h)),
        out_shape=jax.ShapeDtypeStruct((T, MLA_HEADS * V_DIM), BF16),
        scratch_shapes=[pltpu.VMEM((nchain, tq, tk), F32), pltpu.VMEM((nchain, tq, tk), F32)],
        compiler_params=_cparams(("parallel", "parallel")),
        name="mla_attention",
    )(q, k, v)


def _alibi_slope(h):
    return 2.0 ** (-8.0 * (h + 1) / WIN_HEADS)


def _win_kernel(S, tq, sink_ref, q_ref, kp_ref, kc_ref, kn_ref, vp_ref, vc_ref, vn_ref, o_ref):
    i = pl.program_id(1)
    G = WIN_HEADS // WIN_KV_HEADS
    scale = WIN_HD ** -0.5
    kb = jnp.concatenate([kp_ref[...], kc_ref[...], kn_ref[...]], axis=0)
    vb = jnp.concatenate([vp_ref[...], vc_ref[...], vn_ref[...]], axis=0)
    rows = lax.broadcasted_iota(jnp.int32, (G * BLOCK, 3 * BLOCK), 0)
    cols = lax.broadcasted_iota(jnp.int32, (G * BLOCK, 3 * BLOCK), 1)
    qoff = rows % BLOCK
    dist = jnp.abs(qoff + BLOCK - cols)
    distf = dist.astype(F32)
    gidx = rows // BLOCK
    for j in range(tq // BLOCK):
        kstart = i * tq + (j - 1) * BLOCK
        kpos = kstart + cols
        valid = (dist <= WINDOW) & (kpos >= 0) & (kpos < S)
        for kvh in range(WIN_KV_HEADS):
            qs = jnp.concatenate(
                [q_ref[j * BLOCK:(j + 1) * BLOCK, (kvh * G + g) * WIN_HD:(kvh * G + g + 1) * WIN_HD]
                 for g in range(G)], axis=0)
            kk = kb[j * BLOCK:(j + 3) * BLOCK, kvh * WIN_HD:(kvh + 1) * WIN_HD]
            vv = vb[j * BLOCK:(j + 3) * BLOCK, kvh * WIN_HD:(kvh + 1) * WIN_HD]
            s = lax.dot_general(qs, kk, (((1,), (1,)), ((), ())), preferred_element_type=F32) * scale
            slope = jnp.zeros_like(distf)
            sink = jnp.zeros((G * BLOCK, 1), F32)
            for g in range(G):
                hsel = gidx == g
                slope = jnp.where(hsel, _alibi_slope(kvh * G + g), slope)
                sink = jnp.where(hsel[:, :1], sink_ref[kvh * G + g], sink)
            s = jnp.where(valid, s - slope * distf, -jnp.inf)
            m = jnp.maximum(jnp.max(s, axis=1, keepdims=True), sink)
            e = jnp.exp(s - m)
            p = e / (jnp.sum(e, axis=1, keepdims=True) + jnp.exp(sink - m))
            o = jnp.dot(p.astype(BF16), vv, preferred_element_type=F32)
            for g in range(G):
                c0 = (kvh * G + g) * WIN_HD
                o_ref[j * BLOCK:(j + 1) * BLOCK, c0:c0 + WIN_HD] = (
                    o[g * BLOCK:(g + 1) * BLOCK].astype(o_ref.dtype))


def window_attention(zw, sink, B, S, tq=512):
    T = B * S
    tq = _tile(S, tq)
    nq = S // tq
    r = tq // BLOCK
    nb = S // BLOCK
    KW = WIN_KV_HEADS * WIN_HD
    kcol = (WIN_HEADS * WIN_HD + MEM_HEADS * MEM_HD) // KW
    vcol = kcol + 1

    def prev_idx(b, i, col):
        return (b * nb + jnp.maximum(i * r - 1, 0), col)

    def next_idx(b, i, col):
        return (b * nb + jnp.minimum((i + 1) * r, nb - 1), col)

    in_specs = [
        pl.BlockSpec(memory_space=pltpu.SMEM),
        pl.BlockSpec((tq, WIN_HEADS * WIN_HD), lambda b, i: (b * nq + i, 0)),
        pl.BlockSpec((BLOCK, KW), lambda b, i: prev_idx(b, i, kcol)),
        pl.BlockSpec((tq, KW), lambda b, i: (b * nq + i, kcol)),
        pl.BlockSpec((BLOCK, KW), lambda b, i: next_idx(b, i, kcol)),
        pl.BlockSpec((BLOCK, KW), lambda b, i: prev_idx(b, i, vcol)),
        pl.BlockSpec((tq, KW), lambda b, i: (b * nq + i, vcol)),
        pl.BlockSpec((BLOCK, KW), lambda b, i: next_idx(b, i, vcol)),
    ]
    return pl.pallas_call(
        functools.partial(_win_kernel, S, tq),
        grid=(B, nq),
        in_specs=in_specs,
        out_specs=pl.BlockSpec((tq, WIN_HEADS * WIN_HD), lambda b, i: (b * nq + i, 0)),
        out_shape=jax.ShapeDtypeStruct((T, WIN_HEADS * WIN_HD), BF16),
        compiler_params=_cparams(("parallel", "parallel")),
        name="window_attention",
    )(sink, zw, zw, zw, zw, zw, zw, zw)


def _mem_kernel(q_ref, kv_ref, o_ref):
    scale = MEM_HD ** -0.5
    W = MEM_HEADS * MEM_HD
    for h in range(MEM_HEADS):
        q = q_ref[:, h * MEM_HD:(h + 1) * MEM_HD]
        k = kv_ref[:, h * MEM_HD:(h + 1) * MEM_HD]
        v = kv_ref[:, W + h * MEM_HD:W + (h + 1) * MEM_HD]
        s = lax.dot_general(q, k, (((1,), (1,)), ((), ())), preferred_element_type=F32) * scale
        m = jnp.max(s, axis=1, keepdims=True)
        e = jnp.exp(s - m)
        p = e / jnp.sum(e, axis=1, keepdims=True)
        o_ref[:, h * MEM_HD:(h + 1) * MEM_HD] = jnp.dot(
            p.astype(BF16), v, preferred_element_type=F32).astype(o_ref.dtype)


def memory_attention(zw, mem_kv, B, S, tq=512):
    T = B * S
    tq = _tile(S, tq)
    nq = S // tq
    M = mem_kv.shape[0] // B
    W = MEM_HEADS * MEM_HD
    return pl.pallas_call(
        _mem_kernel,
        grid=(B, nq),
        in_specs=[pl.BlockSpec((tq, W), lambda b, i: (b * nq + i, 1)),
                  pl.BlockSpec((M, 2 * W), lambda b, i: (b, 0))],
        out_specs=pl.BlockSpec((tq, W), lambda b, i: (b * nq + i, 0)),
        out_shape=jax.ShapeDtypeStruct((T, W), BF16),
        compiler_params=_cparams(("parallel", "parallel")),
        name="memory_attention",
    )(zw, mem_kv)


def _merge_kernel(h_ref, a_ref, b_ref, c_ref, wga_ref, wgb_ref, wgc_ref,
                  wa_ref, wb_ref, wc_ref, o_ref):
    h = h_ref[...]

    def branch(x_ref, wg_ref, w_ref):
        gate = jax.nn.sigmoid(jnp.dot(h, wg_ref[...], preferred_element_type=F32))
        return gate * jnp.dot(x_ref[...], w_ref[...], preferred_element_type=F32)

    o_ref[...] = (branch(a_ref, wga_ref, wa_ref) + branch(b_ref, wgb_ref, wb_ref)
                  + branch(c_ref, wgc_ref, wc_ref)).astype(o_ref.dtype)


def gated_merge(h, a, b, c, w_g, w_a, w_b, w_c, tm=1024, tn=256):
    T, D = h.shape
    tm, tn = _tile(T, tm), _tile(D, tn)
    nj = D // tn
    act = lambda w: pl.BlockSpec((tm, w), lambda i, j: (i, 0))
    wg = lambda br: pl.BlockSpec((D, tn), lambda i, j: (0, br * nj + j))
    wx = lambda k: pl.BlockSpec((k, tn), lambda i, j: (0, j))
    return pl.pallas_call(
        _merge_kernel,
        grid=(T // tm, nj),
        in_specs=[act(D), act(a.shape[1]), act(b.shape[1]), act(c.shape[1]),
                  wg(0), wg(1), wg(2),
                  wx(a.shape[1]), wx(b.shape[1]), wx(c.shape[1])],
        out_specs=pl.BlockSpec((tm, tn), lambda i, j: (i, j)),
        out_shape=jax.ShapeDtypeStruct((T, D), BF16),
        compiler_params=_cparams(("parallel", "parallel")),
        name="gated_merge",
    )(h, a, b, c, w_g, w_g, w_g, w_a, w_b, w_c)


def _outproj_kernel(m_ref, w_ref, x_ref, gpost_ref, gnext_ref, xo_ref, ho_ref):
    y = jnp.dot(m_ref[...], w_ref[...], preferred_element_type=F32)
    x_new = x_ref[...] + _rms(y, gpost_ref[...])
    xo_ref[...] = x_new
    ho_ref[...] = _rms(x_new, gnext_ref[...]).astype(ho_ref.dtype)


def out_proj(merged, w_out, x, g_post, g_next, tm=512):
    T, D = x.shape
    tm = _tile(T, tm)
    row = lambda: pl.BlockSpec((tm, D), lambda i: (i, 0))
    vec = lambda: pl.BlockSpec((1, D), lambda i: (0, 0))
    return pl.pallas_call(
        _outproj_kernel,
        grid=(T // tm,),
        in_specs=[row(), pl.BlockSpec((D, D), lambda i: (0, 0)), row(), vec(), vec()],
        out_specs=[row(), row()],
        out_shape=[jax.ShapeDtypeStruct((T, D), F32), jax.ShapeDtypeStruct((T, D), BF16)],
        compiler_params=_cparams(("parallel",)),
        name="out_proj",
    )(merged, w_out, x, g_post.reshape(1, D), g_next.reshape(1, D))


def _ffn_kernel(h_ref, wu_ref, wd_ref, x_ref, gpost_ref, gnext_ref, xo_ref, ho_ref, acc_ref):
    k = pl.program_id(1)

    @pl.when(k == 0)
    def _():
        acc_ref[...] = jnp.zeros_like(acc_ref)

    u = jnp.dot(h_ref[...], wu_ref[...], preferred_element_type=F32)
    u = jnp.square(jnp.maximum(u, 0.0)).astype(BF16)
    acc_ref[...] += jnp.dot(u, wd_ref[...], preferred_element_type=F32)

    @pl.when(k == pl.num_programs(1) - 1)
    def _():
        x_new = x_ref[...] + _rms(acc_ref[...], gpost_ref[...])
        xo_ref[...] = x_new
        ho_ref[...] = _rms(x_new, gnext_ref[...]).astype(ho_ref.dtype)


def ffn(h, w_up, w_down, x, g_post, g_next, tm=512, tf=512):
    T, D = x.shape
    Fd = w_up.shape[1]
    tm, tf = _tile(T, tm), _tile(Fd, tf)
    row = lambda: pl.BlockSpec((tm, D), lambda i, k: (i, 0))
    vec = lambda: pl.BlockSpec((1, D), lambda i, k: (0, 0))
    return pl.pallas_call(
        _ffn_kernel,
        grid=(T // tm, Fd // tf),
        in_specs=[row(),
                  pl.BlockSpec((D, tf), lambda i, k: (0, k)),
                  pl.BlockSpec((tf, D), lambda i, k: (k, 0)),
                  row(), vec(), vec()],
        out_specs=[row(), row()],
        out_shape=[jax.ShapeDtypeStruct((T, D), F32), jax.ShapeDtypeStruct((T, D), BF16)],
        scratch_shapes=[pltpu.VMEM((tm, D), F32)],
        compiler_params=_cparams(("parallel", "arbitrary")),
        name="ffn",
    )(h, w_up, w_down, x, g_post.reshape(1, D), g_next.reshape(1, D))


def _prep_layer(w_in, w_uq, w_ukv):
    D = w_in.shape[0]
    o_lat = Q_LORA + KV_LORA + QK_ROPE
    o_qw = o_lat + WIN_HEADS * WIN_HD
    o_kw = o_qw + WIN_KV_HEADS * WIN_HD
    o_vw = o_kw + WIN_KV_HEADS * WIN_HD
    o_qm = o_vw + MEM_HEADS * MEM_HD
    w_lat = jnp.concatenate([w_in[:, :o_lat], jnp.zeros((D, LAT_W - o_lat), w_in.dtype)], axis=1)
    w_zw = jnp.concatenate([w_in[:, o_lat:o_qw], w_in[:, o_vw:o_qm],
                            w_in[:, o_qw:o_kw], w_in[:, o_kw:o_vw]], axis=1)
    w_g = w_in[:, o_qm:]
    wq = w_uq.reshape(Q_LORA, MLA_HEADS, QK_NOPE + QK_ROPE)
    wq = jnp.pad(wq, ((0, 0), (0, 0), (0, QK_PAD - QK_NOPE - QK_ROPE))).reshape(Q_LORA, MLA_HEADS * QK_PAD)
    wkv = w_ukv.reshape(KV_LORA, MLA_HEADS, QK_NOPE + V_DIM)
    wkv = jnp.concatenate([wkv[:, :, :QK_NOPE].reshape(KV_LORA, -1),
                           wkv[:, :, QK_NOPE:].reshape(KV_LORA, -1)], axis=1)
    return (w_lat.astype(BF16), w_zw.astype(BF16), w_g.astype(BF16),
            wq.astype(BF16), wkv.astype(BF16))


def kernel(x_prompt, x_sample, mem_prompt, mem_sample, g_attn_pre, g_attn_post, w_in, g_q_lat, g_kv_lat, w_uq, w_ukv, g_mem, w_mem_kv, sink_logit, w_branch_mla, w_branch_win, w_branch_mem, w_out, g_ffn_pre, g_ffn_post, w_ff_up, w_ff_down):
    depth = w_in.shape[0]
    scale = (QK_NOPE + QK_ROPE) ** -0.5 * math.log2(math.e)
    layers = []
    for l in range(depth):
        layers.append(_prep_layer(w_in[l], w_uq[l], w_ukv[l]) + (
            w_mem_kv[l].astype(BF16), w_branch_mla[l].astype(BF16), w_branch_win[l].astype(BF16),
            w_branch_mem[l].astype(BF16), w_out[l].astype(BF16),
            w_ff_up[l].astype(BF16), w_ff_down[l].astype(BF16)))

    def run(x3, mem3):
        B, S, D = x3.shape
        x = x3.reshape(B * S, D)
        mem = mem3.reshape(B * mem3.shape[1], D)
        h = rmsnorm_bf16(x, g_attn_pre[0])
        for l in range(depth):
            (w_lat, w_zw, w_g, wq, wkv, wmem, wa, wb, wc, wo, wup, wdn) = layers[l]
            z_lat = matmul(h, w_lat, F32, name="in_proj_latent")
            zw = matmul(h, w_zw, BF16, tn=w_zw.shape[1] // 2, name="in_proj_qkv")
            q = mla_q_proj(z_lat, g_q_lat[l], wq, S, scale)
            k, v = mla_kv_proj(z_lat, g_kv_lat[l], wkv, S)
            a = mla_attention(q, k, v, B, S)
            b = window_attention(zw, sink_logit[l], B, S)
            mem_kv = norm_matmul(mem, g_mem[l], wmem, BF16)
            c = memory_attention(zw, mem_kv, B, S)
            merged = gated_merge(h, a, b, c, w_g, wa, wb, wc)
            x, h = out_proj(merged, wo, x, g_attn_post[l], g_ffn_pre[l])
            g_next = g_attn_pre[l + 1] if l + 1 < depth else g_attn_pre[l]
            x, h = ffn(h, wup, wdn, x, g_ffn_post[l], g_next)
        return x.reshape(B, S, D)

    return (run(x_prompt, mem_prompt), run(x_sample, mem_sample))
```

```python
import functools
import math

import jax
import jax.numpy as jnp
from jax import lax
from jax.experimental import pallas as pl
from jax.experimental.pallas import tpu as pltpu

F32 = jnp.float32
BF16 = jnp.bfloat16

EPS = 1e-6
ROPE_THETA = 10000.0
LOG2E = math.log2(math.e)

MLA_HEADS = 8
QK_NOPE = 128
QK_ROPE = 64
V_DIM = 128
Q_LORA = 512
KV_LORA = 512
WIN_HEADS = 8
WIN_KV_HEADS = 2
WIN_HD = 128
WINDOW = 128
BLOCK = 128
MEM_HEADS = 4
MEM_HD = 256
N_BRANCH = 3

LANES = 128
QK_PAD = 2 * LANES
LAT_W = Q_LORA + KV_LORA + LANES

MERGE_TN = 256
FFN_TF = 512
VMEM_LIMIT = 56 * 1024 * 1024


def _cparams(sem):
    return pltpu.CompilerParams(dimension_semantics=sem, vmem_limit_bytes=VMEM_LIMIT)


def _rms(x, g):
    r = lax.rsqrt(jnp.mean(x * x, axis=-1, keepdims=True) + EPS)
    return (x * r) * g


def _tile(n, pref):
    t = min(n, pref)
    assert n % t == 0, (n, t)
    return t


def _vec_spec(n, l, nargs):
    if nargs == 1:
        return pl.BlockSpec((None, 1, n), lambda i: (l, 0, 0))
    return pl.BlockSpec((None, 1, n), lambda i, j: (l, 0, 0))


def _norm_kernel(x_ref, g_ref, o_ref):
    o_ref[...] = _rms(x_ref[...], g_ref[...]).astype(o_ref.dtype)


def rmsnorm_bf16(x, g, l):
    T, D = x.shape
    tm = _tile(T, 512)
    return pl.pallas_call(
        _norm_kernel,
        grid=(T // tm,),
        in_specs=[pl.BlockSpec((tm, D), lambda i: (i, 0)), _vec_spec(D, l, 1)],
        out_specs=pl.BlockSpec((tm, D), lambda i: (i, 0)),
        out_shape=jax.ShapeDtypeStruct((T, D), BF16),
        compiler_params=_cparams(("parallel",)),
        name="rmsnorm",
    )(x, g)


def _mm_kernel(a_ref, w_ref, o_ref):
    o_ref[...] = jnp.dot(a_ref[...], w_ref[...], preferred_element_type=F32).astype(o_ref.dtype)


def matmul(a, w, l, out_dtype, tm=1024, tn=None, name="matmul"):
    T, K = a.shape
    N = w.shape[2]
    tm = _tile(T, tm)
    tn = N if tn is None else _tile(N, tn)
    return pl.pallas_call(
        _mm_kernel,
        grid=(T // tm, N // tn),
        in_specs=[pl.BlockSpec((tm, K), lambda i, j: (i, 0)),
                  pl.BlockSpec((None, K, tn), lambda i, j: (l, 0, j))],
        out_specs=pl.BlockSpec((tm, tn), lambda i, j: (i, j)),
        out_shape=jax.ShapeDtypeStruct((T, N), out_dtype),
        compiler_params=_cparams(("parallel", "parallel")),
        name=name,
    )(a, w)


def _norm_mm_kernel(x_ref, g_ref, w_ref, o_ref):
    h = _rms(x_ref[...], g_ref[...]).astype(BF16)
    o_ref[...] = jnp.dot(h, w_ref[...], preferred_element_type=F32).astype(o_ref.dtype)


def norm_matmul(x, g, w, l, out_dtype, tm=256):
    T, K = x.shape
    N = w.shape[2]
    tm = _tile(T, tm)
    return pl.pallas_call(
        _norm_mm_kernel,
        grid=(T // tm,),
        in_specs=[pl.BlockSpec((tm, K), lambda i: (i, 0)),
                  _vec_spec(K, l, 1),
                  pl.BlockSpec((None, K, N), lambda i: (l, 0, 0))],
        out_specs=pl.BlockSpec((tm, N), lambda i: (i, 0)),
        out_shape=jax.ShapeDtypeStruct((T, N), out_dtype),
        compiler_params=_cparams(("parallel",)),
        name="norm_matmul",
    )(x, g, w)


def _rope_tables(S, scale):
    half = QK_ROPE // 2
    inv = 1.0 / (ROPE_THETA ** (jnp.arange(0, QK_ROPE, 2, dtype=F32) / QK_ROPE))
    ang = jnp.arange(S, dtype=F32)[:, None] * inv[None, :]
    cos, sin = jnp.cos(ang) * scale, jnp.sin(ang) * scale
    z32 = jnp.zeros((S, half), F32)
    z64 = jnp.zeros((S, LANES - QK_ROPE), F32)
    cos_t = jnp.concatenate([cos, cos, z64], axis=1)
    sin_lo = jnp.concatenate([-sin, z32, z64], axis=1)
    sin_hi = jnp.concatenate([z32, sin, z64], axis=1)
    return cos_t, sin_lo, sin_hi


def _rope(v, cos_t, sin_lo, sin_hi):
    half = QK_ROPE // 2
    return (v * cos_t + pltpu.roll(v, half, axis=1) * sin_hi
            + pltpu.roll(v, LANES - half, axis=1) * sin_lo)


def _uq_kernel(scale, c_ref, g_ref, w_ref, cos_ref, slo_ref, shi_ref, o_ref):
    cn = _rms(c_ref[...], g_ref[...]).astype(BF16)
    q = jnp.dot(cn, w_ref[...], preferred_element_type=F32)
    cos_t, sin_lo, sin_hi = cos_ref[...], slo_ref[...], shi_ref[...]
    for h in range(MLA_HEADS):
        base = h * QK_PAD
        o_ref[:, base:base + LANES] = (q[:, base:base + LANES] * scale).astype(o_ref.dtype)
        r = q[:, base + LANES:base + QK_PAD]
        o_ref[:, base + LANES:base + QK_PAD] = _rope(r, cos_t, sin_lo, sin_hi).astype(o_ref.dtype)


def mla_q_proj(z_lat, g_q, w_qm, l, S, scale, tm=512):
    T = z_lat.shape[0]
    tm = _tile(S, tm)
    nS = S // tm
    cos_t, sin_lo, sin_hi = _rope_tables(S, scale)
    tab = pl.BlockSpec((tm, LANES), lambda i: (i % nS, 0))
    N = MLA_HEADS * QK_PAD
    return pl.pallas_call(
        functools.partial(_uq_kernel, scale),
        grid=(T // tm,),
        in_specs=[pl.BlockSpec((tm, Q_LORA), lambda i: (i, 0)),
                  _vec_spec(Q_LORA, l, 1),
                  pl.BlockSpec((None, Q_LORA, N), lambda i: (l, 0, 0)),
                  tab, tab, tab],
        out_specs=pl.BlockSpec((tm, N), lambda i: (i, 0)),
        out_shape=jax.ShapeDtypeStruct((T, N), BF16),
        compiler_params=_cparams(("parallel",)),
        name="mla_q_proj",
    )(z_lat, g_q, w_qm, cos_t, sin_lo, sin_hi)


def _ukv_kernel(c_ref, kr_ref, g_ref, w_ref, cos_ref, slo_ref, shi_ref, k_ref, v_ref):
    cn = _rms(c_ref[...], g_ref[...]).astype(BF16)
    kv = jnp.dot(cn, w_ref[...], preferred_element_type=F32)
    kr = _rope(kr_ref[...], cos_ref[...], slo_ref[...], shi_ref[...]).astype(k_ref.dtype)
    nk = MLA_HEADS * QK_NOPE
    lane = lax.broadcasted_iota(jnp.int32, (kv.shape[0], LANES), 1)
    ones_col = jnp.where(lane == 0, 1.0, 0.0).astype(v_ref.dtype)
    for h in range(MLA_HEADS):
        base = h * QK_PAD
        k_ref[:, base:base + LANES] = kv[:, h * QK_NOPE:(h + 1) * QK_NOPE].astype(k_ref.dtype)
        k_ref[:, base + LANES:base + QK_PAD] = kr
        v_ref[:, base:base + LANES] = kv[:, nk + h * V_DIM:nk + (h + 1) * V_DIM].astype(v_ref.dtype)
        v_ref[:, base + LANES:base + QK_PAD] = ones_col


def mla_kv_proj(z_lat, g_kv, w_kv, l, S, tm=512):
    T = z_lat.shape[0]
    tm = _tile(S, tm)
    nS = S // tm
    cos_t, sin_lo, sin_hi = _rope_tables(S, 1.0)
    tab = pl.BlockSpec((tm, LANES), lambda i: (i % nS, 0))
    NK = MLA_HEADS * QK_PAD
    NV = MLA_HEADS * QK_PAD
    kr_blk = (Q_LORA + KV_LORA) // LANES
    return pl.pallas_call(
        _ukv_kernel,
        grid=(T // tm,),
        in_specs=[pl.BlockSpec((tm, KV_LORA), lambda i: (i, 1)),
                  pl.BlockSpec((tm, LANES), lambda i: (i, kr_blk)),
                  _vec_spec(KV_LORA, l, 1),
                  pl.BlockSpec((None, KV_LORA, MLA_HEADS * (QK_NOPE + V_DIM)), lambda i: (l, 0, 0)),
                  tab, tab, tab],
        out_specs=[pl.BlockSpec((tm, NK), lambda i: (i, 0)),
                   pl.BlockSpec((tm, NV), lambda i: (i, 0))],
        out_shape=[jax.ShapeDtypeStruct((T, NK), BF16),
                   jax.ShapeDtypeStruct((T, NV), BF16)],
        compiler_params=_cparams(("parallel",)),
        name="mla_kv_proj",
    )(z_lat, z_lat, g_kv, w_kv, cos_t, sin_lo, sin_hi)


def _mla_kernel(tq, tk, nchain, unroll, q_ref, k_ref, v_ref, o_ref, sa_ref, sb_ref):
    S = q_ref.shape[0]
    nq, nk = S // (tq * nchain), S // tk
    nsteps = nq * nk
    assert nsteps % unroll == 0 and unroll % 2 == 0

    def rows(qi, c):
        return pl.ds(pl.multiple_of((qi * nchain + c) * tq, tq), tq)

    def scores(t, dst_ref):
        qi = t // nk
        ki = t - qi * nk
        k = k_ref[pl.ds(pl.multiple_of(ki * tk, tk), tk), :]
        for c in range(nchain):
            dst_ref[c] = lax.dot_general(q_ref[rows(qi, c), :], k, (((1,), (1,)), ((), ())),
                                         preferred_element_type=F32)

    def step(t, cur_ref, nxt_ref, carry):
        t_next = jnp.where(t + 1 == nsteps, 0, t + 1)
        scores(t_next, nxt_ref)
        qi = t // nk
        ki = t - qi * nk
        v = v_ref[pl.ds(pl.multiple_of(ki * tk, tk), tk), :]
        out = []
        for c in range(nchain):
            m, acc = carry[c]
            m = jnp.where(ki == 0, -jnp.inf, m)
            s = cur_ref[c]
            m_new = jnp.maximum(m, jnp.max(s, axis=1, keepdims=True))
            alpha = jnp.exp2(m - m_new)
            p = jnp.exp2(s - m_new).astype(BF16)
            acc = alpha * acc + jnp.dot(p, v, preferred_element_type=F32)
            o_ref[rows(qi, c), :] = (acc[:, :V_DIM] / acc[:, V_DIM:V_DIM + 1]).astype(o_ref.dtype)
            out.append((m_new, acc))
        return tuple(out)

    scores(0, sa_ref)

    def body(i, carry):
        for u in range(unroll):
            cur, nxt = (sa_ref, sb_ref) if u % 2 == 0 else (sb_ref, sa_ref)
            carry = step(unroll * i + u, cur, nxt, carry)
        return carry

    init = tuple((jnp.full((tq, 1), -jnp.inf, F32), jnp.zeros((tq, QK_PAD), F32))
                 for _ in range(nchain))
    lax.fori_loop(0, nsteps // unroll, body, init)


def mla_attention(q, k, v, B, S, tq=256, tk=2048, nchain=2):
    T = B * S
    tq, tk = _tile(S, tq), _tile(S, tk)
    assert S % (tq * nchain) == 0
    nsteps = (S // (tq * nchain)) * (S // tk)
    unroll = 4 if nsteps % 4 == 0 else 2
    return pl.pallas_call(
        functools.partial(_mla_kernel, tq, tk, nchain, unroll),
        grid=(B, MLA_HEADS),
        in_specs=[pl.BlockSpec((S, QK_PAD), lambda b, h: (b, h)),
                  pl.BlockSpec((S, QK_PAD), lambda b, h: (b, h)),
                  pl.BlockSpec((S, QK_PAD), lambda b, h: (b, h))],
        out_specs=pl.BlockSpec((S, V_DIM), lambda b, h: (b, h)),
        out_shape=jax.ShapeDtypeStruct((T, MLA_HEADS * V_DIM), BF16),
        scratch_shapes=[pltpu.VMEM((nchain, tq, tk), F32), pltpu.VMEM((nchain, tq, tk), F32)],
        compiler_params=_cparams(("parallel", "parallel")),
        name="mla_attention",
    )(q, k, v)


def _window_bias_table():
    G = WIN_HEADS // WIN_KV_HEADS
    qoff = jnp.arange(BLOCK)[:, None]
    col = jnp.arange(3 * BLOCK)[None, :]
    dist = jnp.abs(qoff + BLOCK - col)
    slopes = jnp.asarray([2.0 ** (-8.0 * (h + 1) / WIN_HEADS) for h in range(WIN_HEADS)], F32)
    bias = -slopes[:, None, None] * dist.astype(F32)[None] * LOG2E
    bias = jnp.where((dist <= WINDOW)[None], bias, -jnp.inf)
    return bias.reshape(WIN_KV_HEADS, G * BLOCK, 3 * BLOCK)


def _win_kernel(nb, tq, bias_ref, sink_ref, q_ref, kp_ref, kc_ref, kn_ref,
                vp_ref, vc_ref, vn_ref, o_ref):
    i = pl.program_id(1)
    G = WIN_HEADS // WIN_KV_HEADS
    r = tq // BLOCK
    c_log2 = WIN_HD ** -0.5 * LOG2E
    kb = jnp.concatenate([kp_ref[...], kc_ref[...], kn_ref[...]], axis=0)
    vb = jnp.concatenate([vp_ref[...], vc_ref[...], vn_ref[...]], axis=0)
    col = lax.broadcasted_iota(jnp.int32, (1, 3 * BLOCK), 1)
    gidx = lax.broadcasted_iota(jnp.int32, (G * BLOCK, 1), 0) // BLOCK
    sinks = []
    for kvh in range(WIN_KV_HEADS):
        sk = jnp.zeros((G * BLOCK, 1), F32)
        for g in range(G):
            sk = jnp.where(gidx == g, sink_ref[kvh * G + g] * LOG2E, sk)
        sinks.append(sk)
    for j in range(r):
        gb = i * r + j
        off_seq = ((gb == 0) & (col < BLOCK)) | ((gb == nb - 1) & (col >= 2 * BLOCK))
        edge = jnp.where(off_seq, -jnp.inf, 0.0)
        for kvh in range(WIN_KV_HEADS):
            qs = jnp.concatenate(
                [q_ref[j * BLOCK:(j + 1) * BLOCK, (kvh * G + g) * WIN_HD:(kvh * G + g + 1) * WIN_HD]
                 for g in range(G)], axis=0)
            kk = kb[j * BLOCK:(j + 3) * BLOCK, kvh * WIN_HD:(kvh + 1) * WIN_HD]
            vv = vb[j * BLOCK:(j + 3) * BLOCK, kvh * WIN_HD:(kvh + 1) * WIN_HD]
            s = lax.dot_general(qs, kk, (((1,), (1,)), ((), ())), preferred_element_type=F32)
            s = s * c_log2 + bias_ref[kvh] + edge
            sink = sinks[kvh]
            m = jnp.maximum(jnp.max(s, axis=1, keepdims=True), sink)
            e = jnp.exp2(s - m)
            d = jnp.sum(e, axis=1, keepdims=True) + jnp.exp2(sink - m)
            o = jnp.dot(e.astype(BF16), vv, preferred_element_type=F32) / d
            for g in range(G):
                c0 = (kvh * G + g) * WIN_HD
                o_ref[j * BLOCK:(j + 1) * BLOCK, c0:c0 + WIN_HD] = (
                    o[g * BLOCK:(g + 1) * BLOCK].astype(o_ref.dtype))


def window_attention(zw, sink, B, S, tq=512):
    T = B * S
    tq = _tile(S, tq)
    nq = S // tq
    r = tq // BLOCK
    nb = S // BLOCK
    G = WIN_HEADS // WIN_KV_HEADS
    KW = WIN_KV_HEADS * WIN_HD
    kcol = (WIN_HEADS * WIN_HD + MEM_HEADS * MEM_HD) // KW
    vcol = kcol + 1
    bias = _window_bias_table()

    def prev_idx(b, i, c):
        return (b * nb + jnp.maximum(i * r - 1, 0), c)

    def next_idx(b, i, c):
        return (b * nb + jnp.minimum((i + 1) * r, nb - 1), c)

    in_specs = [
        pl.BlockSpec(bias.shape, lambda b, i: (0, 0, 0)),
        pl.BlockSpec(memory_space=pltpu.SMEM),
        pl.BlockSpec((tq, WIN_HEADS * WIN_HD), lambda b, i: (b * nq + i, 0)),
        pl.BlockSpec((BLOCK, KW), lambda b, i: prev_idx(b, i, kcol)),
        pl.BlockSpec((tq, KW), lambda b, i: (b * nq + i, kcol)),
        pl.BlockSpec((BLOCK, KW), lambda b, i: next_idx(b, i, kcol)),
        pl.BlockSpec((BLOCK, KW), lambda b, i: prev_idx(b, i, vcol)),
        pl.BlockSpec((tq, KW), lambda b, i: (b * nq + i, vcol)),
        pl.BlockSpec((BLOCK, KW), lambda b, i: next_idx(b, i, vcol)),
    ]
    return pl.pallas_call(
        functools.partial(_win_kernel, nb, tq),
        grid=(B, nq),
        in_specs=in_specs,
        out_specs=pl.BlockSpec((tq, WIN_HEADS * WIN_HD), lambda b, i: (b * nq + i, 0)),
        out_shape=jax.ShapeDtypeStruct((T, WIN_HEADS * WIN_HD), BF16),
        compiler_params=_cparams(("parallel", "parallel")),
        name="window_attention",
    )(bias, sink.astype(F32), zw, zw, zw, zw, zw, zw, zw)


def _mem_kernel(q_ref, kv_ref, o_ref):
    scale = MEM_HD ** -0.5
    W = MEM_HEADS * MEM_HD
    for h in range(MEM_HEADS):
        q = q_ref[:, h * MEM_HD:(h + 1) * MEM_HD]
        k = kv_ref[:, h * MEM_HD:(h + 1) * MEM_HD]
        v = kv_ref[:, W + h * MEM_HD:W + (h + 1) * MEM_HD]
        s = lax.dot_general(q, k, (((1,), (1,)), ((), ())), preferred_element_type=F32) * scale
        m = jnp.max(s, axis=1, keepdims=True)
        e = jnp.exp(s - m)
        p = e / jnp.sum(e, axis=1, keepdims=True)
        o_ref[:, h * MEM_HD:(h + 1) * MEM_HD] = jnp.dot(
            p.astype(BF16), v, preferred_element_type=F32).astype(o_ref.dtype)


def memory_attention(zw, mem_kv, B, S, tq=512):
    T = B * S
    tq = _tile(S, tq)
    nq = S // tq
    M = mem_kv.shape[0] // B
    W = MEM_HEADS * MEM_HD
    return pl.pallas_call(
        _mem_kernel,
        grid=(B, nq),
        in_specs=[pl.BlockSpec((tq, W), lambda b, i: (b * nq + i, 1)),
                  pl.BlockSpec((M, 2 * W), lambda b, i: (b, 0))],
        out_specs=pl.BlockSpec((tq, W), lambda b, i: (b * nq + i, 0)),
        out_shape=jax.ShapeDtypeStruct((T, W), BF16),
        compiler_params=_cparams(("parallel", "parallel")),
        name="memory_attention",
    )(zw, mem_kv)


def _merge_kernel(h_ref, a_ref, b_ref, c_ref, wga_ref, wgb_ref, wgc_ref,
                  wa_ref, wb_ref, wc_ref, o_ref):
    h = h_ref[...]

    def branch(x_ref, wg_ref, w_ref):
        gate = jax.nn.sigmoid(jnp.dot(h, wg_ref[...], preferred_element_type=F32))
        return gate * jnp.dot(x_ref[...], w_ref[...], preferred_element_type=F32)

    o_ref[...] = (branch(a_ref, wga_ref, wa_ref) + branch(b_ref, wgb_ref, wb_ref)
                  + branch(c_ref, wgc_ref, wc_ref)).astype(o_ref.dtype)


def gated_merge(h, a, b, c, w_g, w_a, w_b, w_c, l, tm=1024):
    T, D = h.shape
    tm = _tile(T, tm)
    _, nj, _, tn = w_a.shape
    act = lambda w: pl.BlockSpec((tm, w), lambda i, j: (i, 0))
    wg = lambda br: pl.BlockSpec((None, None, D, tn), lambda i, j: (l, br * nj + j, 0, 0))
    wx = lambda k: pl.BlockSpec((None, None, k, tn), lambda i, j: (l, j, 0, 0))
    return pl.pallas_call(
        _merge_kernel,
        grid=(T // tm, nj),
        in_specs=[act(D), act(a.shape[1]), act(b.shape[1]), act(c.shape[1]),
                  wg(0), wg(1), wg(2),
                  wx(a.shape[1]), wx(b.shape[1]), wx(c.shape[1])],
        out_specs=pl.BlockSpec((tm, tn), lambda i, j: (i, j)),
        out_shape=jax.ShapeDtypeStruct((T, D), BF16),
        compiler_params=_cparams(("parallel", "parallel")),
        name="gated_merge",
    )(h, a, b, c, w_g, w_g, w_g, w_a, w_b, w_c)


def _outproj_kernel(nsplit, m_ref, w_ref, x_ref, gpost_ref, gnext_ref, xo_ref, ho_ref):
    rows = m_ref.shape[0] // nsplit
    for p in range(nsplit):
        r = slice(p * rows, (p + 1) * rows)
        y = jnp.dot(m_ref[r, :], w_ref[...], preferred_element_type=F32)
        x_new = x_ref[r, :] + _rms(y, gpost_ref[...])
        xo_ref[r, :] = x_new
        ho_ref[r, :] = _rms(x_new, gnext_ref[...]).astype(ho_ref.dtype)


def out_proj(merged, w_out, x, g_post, g_next, l, tm=512, nsplit=2):
    T, D = x.shape
    tm = _tile(T, tm)
    row = lambda: pl.BlockSpec((tm, D), lambda i: (i, 0))
    return pl.pallas_call(
        functools.partial(_outproj_kernel, nsplit),
        grid=(T // tm,),
        in_specs=[row(), pl.BlockSpec((None, D, D), lambda i: (l, 0, 0)), row(),
                  _vec_spec(D, l, 1), _vec_spec(D, l, 1)],
        out_specs=[row(), row()],
        out_shape=[jax.ShapeDtypeStruct((T, D), F32), jax.ShapeDtypeStruct((T, D), BF16)],
        compiler_params=_cparams(("parallel",)),
        name="out_proj",
    )(merged, w_out, x, g_post, g_next)


def _ffn_kernel(emit_h, h_ref, wu_ref, wd_ref, x_ref, gpost_ref, gnext_ref, xo_ref, *rest):
    ho_ref, acc_ref = rest if emit_h else (None, rest[0])
    k = pl.program_id(1)

    @pl.when(k == 0)
    def _():
        acc_ref[...] = jnp.zeros_like(acc_ref)

    u = jnp.dot(h_ref[...], wu_ref[...], preferred_element_type=F32)
    u = jnp.square(jnp.maximum(u, 0.0)).astype(BF16)
    acc_ref[...] += jnp.dot(u, wd_ref[...], preferred_element_type=F32)

    @pl.when(k == pl.num_programs(1) - 1)
    def _():
        x_new = x_ref[...] + _rms(acc_ref[...], gpost_ref[...])
        xo_ref[...] = x_new
        if emit_h:
            ho_ref[...] = _rms(x_new, gnext_ref[...]).astype(ho_ref.dtype)


def ffn(h, w_up, w_down, x, g_post, g_next, l, l_next, tm=512):
    T, D = x.shape
    _, nf, _, tf = w_up.shape
    tm = _tile(T, tm)
    emit_h = l_next is not None
    row = lambda: pl.BlockSpec((tm, D), lambda i, k: (i, 0))
    out_specs = [row()] + ([row()] if emit_h else [])
    out_shape = [jax.ShapeDtypeStruct((T, D), F32)] + (
        [jax.ShapeDtypeStruct((T, D), BF16)] if emit_h else [])
    res = pl.pallas_call(
        functools.partial(_ffn_kernel, emit_h),
        grid=(T // tm, nf),
        in_specs=[row(),
                  pl.BlockSpec((None, None, D, tf), lambda i, k: (l, k, 0, 0)),
                  pl.BlockSpec((None, tf, D), lambda i, k: (l, k, 0)),
                  row(), _vec_spec(D, l, 2), _vec_spec(D, l_next if emit_h else l, 2)],
        out_specs=out_specs,
        out_shape=out_shape,
        scratch_shapes=[pltpu.VMEM((tm, D), F32)],
        compiler_params=_cparams(("parallel", "arbitrary")),
        name="ffn",
    )(h, w_up, w_down, x, g_post, g_next)
    return (res[0], res[1]) if emit_h else (res[0], None)


def _col_blocks(w, tn):
    L, K, N = w.shape
    return w.reshape(L, K, N // tn, tn).transpose(0, 2, 1, 3)


def _prep_in_proj(w_in):
    L, D, _ = w_in.shape
    o_lat = Q_LORA + KV_LORA + QK_ROPE
    o_qw = o_lat + WIN_HEADS * WIN_HD
    o_kw = o_qw + WIN_KV_HEADS * WIN_HD
    o_vw = o_kw + WIN_KV_HEADS * WIN_HD
    o_qm = o_vw + MEM_HEADS * MEM_HD
    w = w_in.astype(BF16)
    w_lat = jnp.concatenate([w[:, :, :o_lat], jnp.zeros((L, D, LAT_W - o_lat), BF16)], axis=2)
    w_zw = jnp.concatenate([w[:, :, o_lat:o_qw], w[:, :, o_vw:o_qm],
                            w[:, :, o_qw:o_kw], w[:, :, o_kw:o_vw]], axis=2)
    w_g = _col_blocks(w[:, :, o_qm:], MERGE_TN)
    return w_lat, w_zw, w_g


def _prep_mla(w_uq, w_ukv):
    L = w_uq.shape[0]
    wq = w_uq.astype(BF16).reshape(L, Q_LORA, MLA_HEADS, QK_NOPE + QK_ROPE)
    wq = jnp.pad(wq, ((0, 0), (0, 0), (0, 0), (0, QK_PAD - QK_NOPE - QK_ROPE)))
    wq = wq.reshape(L, Q_LORA, MLA_HEADS * QK_PAD)
    wkv = w_ukv.astype(BF16).reshape(L, KV_LORA, MLA_HEADS, QK_NOPE + V_DIM)
    wkv = jnp.concatenate([wkv[..., :QK_NOPE].reshape(L, KV_LORA, -1),
                           wkv[..., QK_NOPE:].reshape(L, KV_LORA, -1)], axis=2)
    return wq, wkv


def kernel(x_prompt, x_sample, mem_prompt, mem_sample, g_attn_pre, g_attn_post, w_in, g_q_lat, g_kv_lat, w_uq, w_ukv, g_mem, w_mem_kv, sink_logit, w_branch_mla, w_branch_win, w_branch_mem, w_out, g_ffn_pre, g_ffn_post, w_ff_up, w_ff_down):
    depth = w_in.shape[0]
    scale = (QK_NOPE + QK_ROPE) ** -0.5 * LOG2E

    w_lat, w_zw, w_g = _prep_in_proj(w_in)
    wq, wkv = _prep_mla(w_uq, w_ukv)
    wmem = w_mem_kv.astype(BF16)
    wa = _col_blocks(w_branch_mla.astype(BF16), MERGE_TN)
    wb = _col_blocks(w_branch_win.astype(BF16), MERGE_TN)
    wc = _col_blocks(w_branch_mem.astype(BF16), MERGE_TN)
    wo = w_out.astype(BF16)
    wup = _col_blocks(w_ff_up.astype(BF16), FFN_TF)
    wdn = w_ff_down.astype(BF16)
    vec = lambda g: g.reshape(depth, 1, g.shape[1])
    g_pre, g_post, g_q, g_kv = vec(g_attn_pre), vec(g_attn_post), vec(g_q_lat), vec(g_kv_lat)
    g_m, g_fpre, g_fpost = vec(g_mem), vec(g_ffn_pre), vec(g_ffn_post)

    def run(x3, mem3):
        B, S, D = x3.shape
        x = x3.reshape(B * S, D)
        mem = mem3.reshape(B * mem3.shape[1], D)
        h = rmsnorm_bf16(x, g_pre, 0)
        for l in range(depth):
            z_lat = matmul(h, w_lat, l, F32, name="in_proj_latent")
            zw = matmul(h, w_zw, l, BF16, tn=w_zw.shape[2] // 2, name="in_proj_qkv")
            q = mla_q_proj(z_lat, g_q, wq, l, S, scale)
            k, v = mla_kv_proj(z_lat, g_kv, wkv, l, S)
            a = mla_attention(q, k, v, B, S)
            b = window_attention(zw, sink_logit[l], B, S)
            mem_kv = norm_matmul(mem, g_m, wmem, l, BF16)
            c = memory_attention(zw, mem_kv, B, S)
            merged = gated_merge(h, a, b, c, w_g, wa, wb, wc, l)
            x, h = out_proj(merged, wo, x, g_post, g_fpre, l)
            x, h = ffn(h, wup, wdn, x, g_fpost, g_pre, l, l + 1 if l + 1 < depth else None)
        return x.reshape(B, S, D)

    return (run(x_prompt, mem_prompt), run(x_sample, mem_sample))
```

```python
import functools
import math

import jax
import jax.numpy as jnp
from jax import lax
from jax.experimental import pallas as pl
from jax.experimental.pallas import tpu as pltpu

F32 = jnp.float32
BF16 = jnp.bfloat16

EPS = 1e-6
ROPE_THETA = 10000.0
LOG2E = math.log2(math.e)

MLA_HEADS = 8
QK_NOPE = 128
QK_ROPE = 64
V_DIM = 128
Q_LORA = 512
KV_LORA = 512
WIN_HEADS = 8
WIN_KV_HEADS = 2
WIN_HD = 128
WINDOW = 128
BLOCK = 128
MEM_HEADS = 4
MEM_HD = 256
N_BRANCH = 3

LANES = 128
QK_PAD = 2 * LANES
LAT_W = Q_LORA + KV_LORA + LANES

MERGE_TN = 256
FFN_TF = 1024
VMEM_LIMIT = 56 * 1024 * 1024


def _cparams(sem):
    return pltpu.CompilerParams(dimension_semantics=sem, vmem_limit_bytes=VMEM_LIMIT)


def _rms(x, g):
    r = lax.rsqrt(jnp.mean(x * x, axis=-1, keepdims=True) + EPS)
    return (x * r) * g


def _tile(n, pref):
    t = min(n, pref)
    assert n % t == 0, (n, t)
    return t


def _vec_spec(n, l, nargs):
    if nargs == 1:
        return pl.BlockSpec((None, 1, n), lambda i: (l, 0, 0))
    return pl.BlockSpec((None, 1, n), lambda i, j: (l, 0, 0))


def _norm_kernel(x_ref, g_ref, o_ref):
    o_ref[...] = _rms(x_ref[...], g_ref[...]).astype(o_ref.dtype)


def rmsnorm_bf16(x, g, l):
    T, D = x.shape
    tm = _tile(T, 512)
    return pl.pallas_call(
        _norm_kernel,
        grid=(T // tm,),
        in_specs=[pl.BlockSpec((tm, D), lambda i: (i, 0)), _vec_spec(D, l, 1)],
        out_specs=pl.BlockSpec((tm, D), lambda i: (i, 0)),
        out_shape=jax.ShapeDtypeStruct((T, D), BF16),
        compiler_params=_cparams(("parallel",)),
        name="rmsnorm",
    )(x, g)


def _mm_kernel(a_ref, w_ref, o_ref):
    o_ref[...] = jnp.dot(a_ref[...], w_ref[...], preferred_element_type=F32).astype(o_ref.dtype)


def matmul(a, w, l, out_dtype, tm=1024, tn=None, name="matmul"):
    T, K = a.shape
    N = w.shape[2]
    tm = _tile(T, tm)
    tn = N if tn is None else _tile(N, tn)
    return pl.pallas_call(
        _mm_kernel,
        grid=(T // tm, N // tn),
        in_specs=[pl.BlockSpec((tm, K), lambda i, j: (i, 0)),
                  pl.BlockSpec((None, K, tn), lambda i, j: (l, 0, j))],
        out_specs=pl.BlockSpec((tm, tn), lambda i, j: (i, j)),
        out_shape=jax.ShapeDtypeStruct((T, N), out_dtype),
        compiler_params=_cparams(("parallel", "parallel")),
        name=name,
    )(a, w)


def _norm_mm_kernel(x_ref, g_ref, w_ref, o_ref):
    h = _rms(x_ref[...], g_ref[...]).astype(BF16)
    o_ref[...] = jnp.dot(h, w_ref[...], preferred_element_type=F32).astype(o_ref.dtype)


def norm_matmul(x, g, w, l, out_dtype, tm=256):
    T, K = x.shape
    N = w.shape[2]
    tm = _tile(T, tm)
    return pl.pallas_call(
        _norm_mm_kernel,
        grid=(T // tm,),
        in_specs=[pl.BlockSpec((tm, K), lambda i: (i, 0)),
                  _vec_spec(K, l, 1),
                  pl.BlockSpec((None, K, N), lambda i: (l, 0, 0))],
        out_specs=pl.BlockSpec((tm, N), lambda i: (i, 0)),
        out_shape=jax.ShapeDtypeStruct((T, N), out_dtype),
        compiler_params=_cparams(("parallel",)),
        name="norm_matmul",
    )(x, g, w)


def _rope_tables(S, scale):
    half = QK_ROPE // 2
    inv = 1.0 / (ROPE_THETA ** (jnp.arange(0, QK_ROPE, 2, dtype=F32) / QK_ROPE))
    ang = jnp.arange(S, dtype=F32)[:, None] * inv[None, :]
    cos, sin = jnp.cos(ang) * scale, jnp.sin(ang) * scale
    z32 = jnp.zeros((S, half), F32)
    z64 = jnp.zeros((S, LANES - QK_ROPE), F32)
    cos_t = jnp.concatenate([cos, cos, z64], axis=1)
    sin_lo = jnp.concatenate([-sin, z32, z64], axis=1)
    sin_hi = jnp.concatenate([z32, sin, z64], axis=1)
    return cos_t, sin_lo, sin_hi


def _rope(v, cos_t, sin_lo, sin_hi):
    half = QK_ROPE // 2
    return (v * cos_t + pltpu.roll(v, half, axis=1) * sin_hi
            + pltpu.roll(v, LANES - half, axis=1) * sin_lo)


def _mla_proj_kernel(scale, nsplit, h_ref, wl_ref, gq_ref, gkv_ref, wq_ref, wkv_ref,
                     qcos_ref, qslo_ref, qshi_ref, kcos_ref, kslo_ref, kshi_ref,
                     q_ref, k_ref, v_ref):
    rows = h_ref.shape[0] // nsplit
    nk = MLA_HEADS * QK_NOPE
    lane = lax.broadcasted_iota(jnp.int32, (rows, LANES), 1)
    ones_col = jnp.where(lane == 0, 1.0, 0.0).astype(v_ref.dtype)
    for p in range(nsplit):
        r = slice(p * rows, (p + 1) * rows)
        z = jnp.dot(h_ref[r, :], wl_ref[...], preferred_element_type=F32)
        cq = _rms(z[:, :Q_LORA], gq_ref[...]).astype(BF16)
        ckv = _rms(z[:, Q_LORA:Q_LORA + KV_LORA], gkv_ref[...]).astype(BF16)
        q = jnp.dot(cq, wq_ref[...], preferred_element_type=F32)
        kv = jnp.dot(ckv, wkv_ref[...], preferred_element_type=F32)
        kr = _rope(z[:, Q_LORA + KV_LORA:], kcos_ref[r, :], kslo_ref[r, :], kshi_ref[r, :])
        kr = kr.astype(k_ref.dtype)
        qcos, qslo, qshi = qcos_ref[r, :], qslo_ref[r, :], qshi_ref[r, :]
        for h in range(MLA_HEADS):
            base = h * QK_PAD
            q_ref[r, base:base + LANES] = (q[:, base:base + LANES] * scale).astype(q_ref.dtype)
            q_ref[r, base + LANES:base + QK_PAD] = _rope(
                q[:, base + LANES:base + QK_PAD], qcos, qslo, qshi).astype(q_ref.dtype)
            k_ref[r, base:base + LANES] = kv[:, h * QK_NOPE:(h + 1) * QK_NOPE].astype(k_ref.dtype)
            k_ref[r, base + LANES:base + QK_PAD] = kr
            v_ref[r, base:base + LANES] = kv[:, nk + h * V_DIM:nk + (h + 1) * V_DIM].astype(v_ref.dtype)
            v_ref[r, base + LANES:base + QK_PAD] = ones_col


def mla_projections(h, w_lat, g_q, g_kv, w_qm, w_kv, l, S, scale, tm=512, nsplit=2):
    T, D = h.shape
    tm = _tile(S, tm)
    nS = S // tm
    q_tabs = _rope_tables(S, scale)
    k_tabs = _rope_tables(S, 1.0)
    tab = pl.BlockSpec((tm, LANES), lambda i: (i % nS, 0))
    N = MLA_HEADS * QK_PAD
    full = lambda a: pl.BlockSpec((None,) + a.shape[1:], lambda i: (l, 0, 0))
    out = pl.BlockSpec((tm, N), lambda i: (i, 0))
    return pl.pallas_call(
        functools.partial(_mla_proj_kernel, scale, nsplit),
        grid=(T // tm,),
        in_specs=[pl.BlockSpec((tm, D), lambda i: (i, 0)), full(w_lat),
                  _vec_spec(Q_LORA, l, 1), _vec_spec(KV_LORA, l, 1), full(w_qm), full(w_kv),
                  tab, tab, tab, tab, tab, tab],
        out_specs=[out, out, out],
        out_shape=[jax.ShapeDtypeStruct((T, N), BF16)] * 3,
        compiler_params=_cparams(("parallel",)),
        name="mla_projections",
    )(h, w_lat, g_q, g_kv, w_qm, w_kv, *q_tabs, *k_tabs)


def _mla_kernel(tq, tk, nchain, unroll, q_ref, k_ref, v_ref, o_ref, sa_ref, sb_ref):
    S = q_ref.shape[0]
    nq, nk = S // (tq * nchain), S // tk
    nsteps = nq * nk
    assert nsteps % unroll == 0 and unroll % 2 == 0

    def rows(qi, c):
        return pl.ds(pl.multiple_of((qi * nchain + c) * tq, tq), tq)

    def scores(t, dst_ref):
        qi = t // nk
        ki = t - qi * nk
        k = k_ref[pl.ds(pl.multiple_of(ki * tk, tk), tk), :]
        for c in range(nchain):
            dst_ref[c] = lax.dot_general(q_ref[rows(qi, c), :], k, (((1,), (1,)), ((), ())),
                                         preferred_element_type=F32)

    def step(t, cur_ref, nxt_ref, carry):
        t_next = jnp.where(t + 1 == nsteps, 0, t + 1)
        scores(t_next, nxt_ref)
        qi = t // nk
        ki = t - qi * nk
        v = v_ref[pl.ds(pl.multiple_of(ki * tk, tk), tk), :]
        out = []
        for c in range(nchain):
            m, acc = carry[c]
            m = jnp.where(ki == 0, -jnp.inf, m)
            s = cur_ref[c]
            m_new = jnp.maximum(m, jnp.max(s, axis=1, keepdims=True))
            alpha = jnp.exp2(m - m_new)
            p = jnp.exp2(s - m_new).astype(BF16)
            acc = alpha * acc + jnp.dot(p, v, preferred_element_type=F32)
            o_ref[rows(qi, c), :] = (acc[:, :V_DIM] / acc[:, V_DIM:V_DIM + 1]).astype(o_ref.dtype)
            out.append((m_new, acc))
        return tuple(out)

    scores(0, sa_ref)

    def body(i, carry):
        for u in range(unroll):
            cur, nxt = (sa_ref, sb_ref) if u % 2 == 0 else (sb_ref, sa_ref)
            carry = step(unroll * i + u, cur, nxt, carry)
        return carry

    init = tuple((jnp.full((tq, 1), -jnp.inf, F32), jnp.zeros((tq, QK_PAD), F32))
                 for _ in range(nchain))
    lax.fori_loop(0, nsteps // unroll, body, init)


def mla_attention(q, k, v, B, S, tq=256, tk=2048, nchain=2):
    T = B * S
    tq, tk = _tile(S, tq), _tile(S, tk)
    assert S % (tq * nchain) == 0
    nsteps = (S // (tq * nchain)) * (S // tk)
    unroll = 4 if nsteps % 4 == 0 else 2
    return pl.pallas_call(
        functools.partial(_mla_kernel, tq, tk, nchain, unroll),
        grid=(B, MLA_HEADS),
        in_specs=[pl.BlockSpec((S, QK_PAD), lambda b, h: (b, h)),
                  pl.BlockSpec((S, QK_PAD), lambda b, h: (b, h)),
                  pl.BlockSpec((S, QK_PAD), lambda b, h: (b, h))],
        out_specs=pl.BlockSpec((S, V_DIM), lambda b, h: (b, h)),
        out_shape=jax.ShapeDtypeStruct((T, MLA_HEADS * V_DIM), BF16),
        scratch_shapes=[pltpu.VMEM((nchain, tq, tk), F32), pltpu.VMEM((nchain, tq, tk), F32)],
        compiler_params=_cparams(("parallel", "parallel")),
        name="mla_attention",
    )(q, k, v)


def _window_bias_table():
    G = WIN_HEADS // WIN_KV_HEADS
    qoff = jnp.arange(BLOCK)[:, None]
    col = jnp.arange(3 * BLOCK)[None, :]
    dist = jnp.abs(qoff + BLOCK - col)
    slopes = jnp.asarray([2.0 ** (-8.0 * (h + 1) / WIN_HEADS) for h in range(WIN_HEADS)], F32)
    bias = -slopes[:, None, None] * dist.astype(F32)[None] * LOG2E
    bias = jnp.where((dist <= WINDOW)[None], bias, -jnp.inf)
    return bias.reshape(WIN_KV_HEADS, G * BLOCK, 3 * BLOCK)


def _win_kernel(nb, tq, bias_ref, sink_ref, q_ref, kp_ref, kc_ref, kn_ref,
                vp_ref, vc_ref, vn_ref, o_ref):
    i = pl.program_id(1)
    G = WIN_HEADS // WIN_KV_HEADS
    r = tq // BLOCK
    c_log2 = WIN_HD ** -0.5 * LOG2E
    kb = jnp.concatenate([kp_ref[...], kc_ref[...], kn_ref[...]], axis=0)
    vb = jnp.concatenate([vp_ref[...], vc_ref[...], vn_ref[...]], axis=0)
    col = lax.broadcasted_iota(jnp.int32, (1, 3 * BLOCK), 1)
    gidx = lax.broadcasted_iota(jnp.int32, (G * BLOCK, 1), 0) // BLOCK
    sinks = []
    for kvh in range(WIN_KV_HEADS):
        sk = jnp.zeros((G * BLOCK, 1), F32)
        for g in range(G):
            sk = jnp.where(gidx == g, sink_ref[kvh * G + g] * LOG2E, sk)
        sinks.append(sk)
    for j in range(r):
        gb = i * r + j
        off_seq = ((gb == 0) & (col < BLOCK)) | ((gb == nb - 1) & (col >= 2 * BLOCK))
        edge = jnp.where(off_seq, -jnp.inf, 0.0)
        for kvh in range(WIN_KV_HEADS):
            qs = jnp.concatenate(
                [q_ref[j * BLOCK:(j + 1) * BLOCK, (kvh * G + g) * WIN_HD:(kvh * G + g + 1) * WIN_HD]
                 for g in range(G)], axis=0)
            kk = kb[j * BLOCK:(j + 3) * BLOCK, kvh * WIN_HD:(kvh + 1) * WIN_HD]
            vv = vb[j * BLOCK:(j + 3) * BLOCK, kvh * WIN_HD:(kvh + 1) * WIN_HD]
            s = lax.dot_general(qs, kk, (((1,), (1,)), ((), ())), preferred_element_type=F32)
            s = s * c_log2 + bias_ref[kvh] + edge
            sink = sinks[kvh]
            m = jnp.maximum(jnp.max(s, axis=1, keepdims=True), sink)
            e = jnp.exp2(s - m)
            d = jnp.sum(e, axis=1, keepdims=True) + jnp.exp2(sink - m)
            o = jnp.dot(e.astype(BF16), vv, preferred_element_type=F32) / d
            for g in range(G):
                c0 = (kvh * G + g) * WIN_HD
                o_ref[j * BLOCK:(j + 1) * BLOCK, c0:c0 + WIN_HD] = (
                    o[g * BLOCK:(g + 1) * BLOCK].astype(o_ref.dtype))


def window_attention(zw, sink, B, S, tq=512):
    T = B * S
    tq = _tile(S, tq)
    nq = S // tq
    r = tq // BLOCK
    nb = S // BLOCK
    G = WIN_HEADS // WIN_KV_HEADS
    KW = WIN_KV_HEADS * WIN_HD
    kcol = (WIN_HEADS * WIN_HD + MEM_HEADS * MEM_HD) // KW
    vcol = kcol + 1
    bias = _window_bias_table()

    def prev_idx(b, i, c):
        return (b * nb + jnp.maximum(i * r - 1, 0), c)

    def next_idx(b, i, c):
        return (b * nb + jnp.minimum((i + 1) * r, nb - 1), c)

    in_specs = [
        pl.BlockSpec(bias.shape, lambda b, i: (0, 0, 0)),
        pl.BlockSpec(memory_space=pltpu.SMEM),
        pl.BlockSpec((tq, WIN_HEADS * WIN_HD), lambda b, i: (b * nq + i, 0)),
        pl.BlockSpec((BLOCK, KW), lambda b, i: prev_idx(b, i, kcol)),
        pl.BlockSpec((tq, KW), lambda b, i: (b * nq + i, kcol)),
        pl.BlockSpec((BLOCK, KW), lambda b, i: next_idx(b, i, kcol)),
        pl.BlockSpec((BLOCK, KW), lambda b, i: prev_idx(b, i, vcol)),
        pl.BlockSpec((tq, KW), lambda b, i: (b * nq + i, vcol)),
        pl.BlockSpec((BLOCK, KW), lambda b, i: next_idx(b, i, vcol)),
    ]
    return pl.pallas_call(
        functools.partial(_win_kernel, nb, tq),
        grid=(B, nq),
        in_specs=in_specs,
        out_specs=pl.BlockSpec((tq, WIN_HEADS * WIN_HD), lambda b, i: (b * nq + i, 0)),
        out_shape=jax.ShapeDtypeStruct((T, WIN_HEADS * WIN_HD), BF16),
        compiler_params=_cparams(("parallel", "parallel")),
        name="window_attention",
    )(bias, sink.astype(F32), zw, zw, zw, zw, zw, zw, zw)


def _mem_kernel(q_ref, kv_ref, o_ref):
    scale = MEM_HD ** -0.5
    W = MEM_HEADS * MEM_HD
    for h in range(MEM_HEADS):
        q = q_ref[:, h * MEM_HD:(h + 1) * MEM_HD]
        k = kv_ref[:, h * MEM_HD:(h + 1) * MEM_HD]
        v = kv_ref[:, W + h * MEM_HD:W + (h + 1) * MEM_HD]
        s = lax.dot_general(q, k, (((1,), (1,)), ((), ())), preferred_element_type=F32) * scale
        m = jnp.max(s, axis=1, keepdims=True)
        e = jnp.exp(s - m)
        p = e / jnp.sum(e, axis=1, keepdims=True)
        o_ref[:, h * MEM_HD:(h + 1) * MEM_HD] = jnp.dot(
            p.astype(BF16), v, preferred_element_type=F32).astype(o_ref.dtype)


def memory_attention(zw, mem_kv, B, S, tq=512):
    T = B * S
    tq = _tile(S, tq)
    nq = S // tq
    M = mem_kv.shape[0] // B
    W = MEM_HEADS * MEM_HD
    return pl.pallas_call(
        _mem_kernel,
        grid=(B, nq),
        in_specs=[pl.BlockSpec((tq, W), lambda b, i: (b * nq + i, 1)),
                  pl.BlockSpec((M, 2 * W), lambda b, i: (b, 0))],
        out_specs=pl.BlockSpec((tq, W), lambda b, i: (b * nq + i, 0)),
        out_shape=jax.ShapeDtypeStruct((T, W), BF16),
        compiler_params=_cparams(("parallel", "parallel")),
        name="memory_attention",
    )(zw, mem_kv)


def _merge_kernel(h_ref, a_ref, b_ref, c_ref, wga_ref, wgb_ref, wgc_ref,
                  wa_ref, wb_ref, wc_ref, o_ref):
    h = h_ref[...]

    def branch(x_ref, wg_ref, w_ref):
        gate = jax.nn.sigmoid(jnp.dot(h, wg_ref[...], preferred_element_type=F32))
        return gate * jnp.dot(x_ref[...], w_ref[...], preferred_element_type=F32)

    o_ref[...] = (branch(a_ref, wga_ref, wa_ref) + branch(b_ref, wgb_ref, wb_ref)
                  + branch(c_ref, wgc_ref, wc_ref)).astype(o_ref.dtype)


def gated_merge(h, a, b, c, w_g, w_a, w_b, w_c, l, tm=1024, tn=MERGE_TN):
    T, D = h.shape
    tm, tn = _tile(T, tm), _tile(D, tn)
    nj = D // tn
    act = lambda w: pl.BlockSpec((tm, w), lambda i, j: (i, 0))
    wg = lambda br: pl.BlockSpec((None, D, tn), lambda i, j: (l, 0, br * nj + j))
    wx = lambda k: pl.BlockSpec((None, k, tn), lambda i, j: (l, 0, j))
    return pl.pallas_call(
        _merge_kernel,
        grid=(T // tm, nj),
        in_specs=[act(D), act(a.shape[1]), act(b.shape[1]), act(c.shape[1]),
                  wg(0), wg(1), wg(2),
                  wx(a.shape[1]), wx(b.shape[1]), wx(c.shape[1])],
        out_specs=pl.BlockSpec((tm, tn), lambda i, j: (i, j)),
        out_shape=jax.ShapeDtypeStruct((T, D), BF16),
        compiler_params=_cparams(("parallel", "parallel")),
        name="gated_merge",
    )(h, a, b, c, w_g, w_g, w_g, w_a, w_b, w_c)


def _outproj_kernel(nsplit, m_ref, w_ref, x_ref, gpost_ref, gnext_ref, xo_ref, ho_ref):
    rows = m_ref.shape[0] // nsplit
    for p in range(nsplit):
        r = slice(p * rows, (p + 1) * rows)
        y = jnp.dot(m_ref[r, :], w_ref[...], preferred_element_type=F32)
        x_new = x_ref[r, :] + _rms(y, gpost_ref[...])
        xo_ref[r, :] = x_new
        ho_ref[r, :] = _rms(x_new, gnext_ref[...]).astype(ho_ref.dtype)


def out_proj(merged, w_out, x, g_post, g_next, l, tm=512, nsplit=2):
    T, D = x.shape
    tm = _tile(T, tm)
    row = lambda: pl.BlockSpec((tm, D), lambda i: (i, 0))
    return pl.pallas_call(
        functools.partial(_outproj_kernel, nsplit),
        grid=(T // tm,),
        in_specs=[row(), pl.BlockSpec((None, D, D), lambda i: (l, 0, 0)), row(),
                  _vec_spec(D, l, 1), _vec_spec(D, l, 1)],
        out_specs=[row(), row()],
        out_shape=[jax.ShapeDtypeStruct((T, D), F32), jax.ShapeDtypeStruct((T, D), BF16)],
        compiler_params=_cparams(("parallel",)),
        name="out_proj",
    )(merged, w_out, x, g_post, g_next)


def _ffn_kernel(emit_h, h_ref, wu_ref, wd_ref, x_ref, gpost_ref, gnext_ref, xo_ref, *rest):
    ho_ref, acc_ref = rest if emit_h else (None, rest[0])
    k = pl.program_id(1)

    @pl.when(k == 0)
    def _():
        acc_ref[...] = jnp.zeros_like(acc_ref)

    u = jnp.dot(h_ref[...], wu_ref[...], preferred_element_type=F32)
    u = jnp.square(jnp.maximum(u, 0.0)).astype(BF16)
    acc_ref[...] += jnp.dot(u, wd_ref[...], preferred_element_type=F32)

    @pl.when(k == pl.num_programs(1) - 1)
    def _():
        x_new = x_ref[...] + _rms(acc_ref[...], gpost_ref[...])
        xo_ref[...] = x_new
        if emit_h:
            ho_ref[...] = _rms(x_new, gnext_ref[...]).astype(ho_ref.dtype)


def ffn(h, w_up, w_down, x, g_post, g_next, l, l_next, tm=512, tf=FFN_TF):
    T, D = x.shape
    Fd = w_up.shape[2]
    tm, tf = _tile(T, tm), _tile(Fd, tf)
    nf = Fd // tf
    emit_h = l_next is not None
    row = lambda: pl.BlockSpec((tm, D), lambda i, k: (i, 0))
    out_specs = [row()] + ([row()] if emit_h else [])
    out_shape = [jax.ShapeDtypeStruct((T, D), F32)] + (
        [jax.ShapeDtypeStruct((T, D), BF16)] if emit_h else [])
    res = pl.pallas_call(
        functools.partial(_ffn_kernel, emit_h),
        grid=(T // tm, nf),
        in_specs=[row(),
                  pl.BlockSpec((None, D, tf), lambda i, k: (l, 0, k)),
                  pl.BlockSpec((None, tf, D), lambda i, k: (l, k, 0)),
                  row(), _vec_spec(D, l, 2), _vec_spec(D, l_next if emit_h else l, 2)],
        out_specs=out_specs,
        out_shape=out_shape,
        scratch_shapes=[pltpu.VMEM((tm, D), F32)],
        compiler_params=_cparams(("parallel", "arbitrary")),
        name="ffn",
    )(h, w_up, w_down, x, g_post, g_next)
    return (res[0], res[1]) if emit_h else (res[0], None)


def _prep_in_proj(w_in):
    L, D, _ = w_in.shape
    o_lat = Q_LORA + KV_LORA + QK_ROPE
    o_qw = o_lat + WIN_HEADS * WIN_HD
    o_kw = o_qw + WIN_KV_HEADS * WIN_HD
    o_vw = o_kw + WIN_KV_HEADS * WIN_HD
    o_qm = o_vw + MEM_HEADS * MEM_HD
    w = w_in.astype(BF16)
    w_lat = jnp.concatenate([w[:, :, :o_lat], jnp.zeros((L, D, LAT_W - o_lat), BF16)], axis=2)
    w_zw = jnp.concatenate([w[:, :, o_lat:o_qw], w[:, :, o_vw:o_qm],
                            w[:, :, o_qw:o_kw], w[:, :, o_kw:o_vw]], axis=2)
    w_g = w[:, :, o_qm:]
    return w_lat, w_zw, w_g


def _prep_mla(w_uq, w_ukv):
    L = w_uq.shape[0]
    wq = w_uq.astype(BF16).reshape(L, Q_LORA, MLA_HEADS, QK_NOPE + QK_ROPE)
    wq = jnp.pad(wq, ((0, 0), (0, 0), (0, 0), (0, QK_PAD - QK_NOPE - QK_ROPE)))
    wq = wq.reshape(L, Q_LORA, MLA_HEADS * QK_PAD)
    wkv = w_ukv.astype(BF16).reshape(L, KV_LORA, MLA_HEADS, QK_NOPE + V_DIM)
    wkv = jnp.concatenate([wkv[..., :QK_NOPE].reshape(L, KV_LORA, -1),
                           wkv[..., QK_NOPE:].reshape(L, KV_LORA, -1)], axis=2)
    return wq, wkv


def kernel(x_prompt, x_sample, mem_prompt, mem_sample, g_attn_pre, g_attn_post, w_in, g_q_lat, g_kv_lat, w_uq, w_ukv, g_mem, w_mem_kv, sink_logit, w_branch_mla, w_branch_win, w_branch_mem, w_out, g_ffn_pre, g_ffn_post, w_ff_up, w_ff_down):
    depth = w_in.shape[0]
    scale = (QK_NOPE + QK_ROPE) ** -0.5 * LOG2E

    w_lat, w_zw, w_g = _prep_in_proj(w_in)
    wq, wkv = _prep_mla(w_uq, w_ukv)
    wmem = w_mem_kv.astype(BF16)
    wa = w_branch_mla.astype(BF16)
    wb = w_branch_win.astype(BF16)
    wc = w_branch_mem.astype(BF16)
    wo = w_out.astype(BF16)
    wup = w_ff_up.astype(BF16)
    wdn = w_ff_down.astype(BF16)
    vec = lambda g: g.reshape(depth, 1, g.shape[1])
    g_pre, g_post, g_q, g_kv = vec(g_attn_pre), vec(g_attn_post), vec(g_q_lat), vec(g_kv_lat)
    g_m, g_fpre, g_fpost = vec(g_mem), vec(g_ffn_pre), vec(g_ffn_post)

    def run(x3, mem3):
        B, S, D = x3.shape
        x = x3.reshape(B * S, D)
        mem = mem3.reshape(B * mem3.shape[1], D)
        h = rmsnorm_bf16(x, g_pre, 0)
        for l in range(depth):
            q, k, v = mla_projections(h, w_lat, g_q, g_kv, wq, wkv, l, S, scale)
            zw = matmul(h, w_zw, l, BF16, tn=w_zw.shape[2] // 2, name="in_proj_qkv")
            a = mla_attention(q, k, v, B, S)
            b = window_attention(zw, sink_logit[l], B, S)
            mem_kv = norm_matmul(mem, g_m, wmem, l, BF16)
            c = memory_attention(zw, mem_kv, B, S)
            merged = gated_merge(h, a, b, c, w_g, wa, wb, wc, l)
            x, h = out_proj(merged, wo, x, g_post, g_fpre, l)
            x, h = ffn(h, wup, wdn, x, g_fpost, g_pre, l, l + 1 if l + 1 < depth else None)
        return x.reshape(B, S, D)

    return (run(x_prompt, mem_prompt), run(x_sample, mem_sample))
```

```python
import functools
import math

import jax
import jax.numpy as jnp
from jax import lax
from jax.experimental import pallas as pl
from jax.experimental.pallas import tpu as pltpu

F32 = jnp.float32
BF16 = jnp.bfloat16

EPS = 1e-6
ROPE_THETA = 10000.0
LOG2E = math.log2(math.e)

MLA_HEADS = 8
QK_NOPE = 128
QK_ROPE = 64
V_DIM = 128
Q_LORA = 512
KV_LORA = 512
WIN_HEADS = 8
WIN_KV_HEADS = 2
WIN_HD = 128
WINDOW = 128
BLOCK = 128
MEM_HEADS = 4
MEM_HD = 256
N_BRANCH = 3

LANES = 128
QK_PAD = 2 * LANES
LAT_W = Q_LORA + KV_LORA + LANES

MERGE_TN = 512
FFN_TF = 1024
VMEM_LIMIT = 56 * 1024 * 1024


def _cparams(sem):
    return pltpu.CompilerParams(dimension_semantics=sem, vmem_limit_bytes=VMEM_LIMIT)


def _rms(x, g):
    r = lax.rsqrt(jnp.mean(x * x, axis=-1, keepdims=True) + EPS)
    return (x * r) * g


def _tile(n, pref):
    t = min(n, pref)
    assert n % t == 0, (n, t)
    return t


def _vec_spec(n, l, nargs):
    if nargs == 1:
        return pl.BlockSpec((None, 1, n), lambda i: (l, 0, 0))
    return pl.BlockSpec((None, 1, n), lambda i, j: (l, 0, 0))


def _norm_kernel(x_ref, g_ref, o_ref):
    o_ref[...] = _rms(x_ref[...], g_ref[...]).astype(o_ref.dtype)


def rmsnorm_bf16(x, g, l):
    T, D = x.shape
    tm = _tile(T, 512)
    return pl.pallas_call(
        _norm_kernel,
        grid=(T // tm,),
        in_specs=[pl.BlockSpec((tm, D), lambda i: (i, 0)), _vec_spec(D, l, 1)],
        out_specs=pl.BlockSpec((tm, D), lambda i: (i, 0)),
        out_shape=jax.ShapeDtypeStruct((T, D), BF16),
        compiler_params=_cparams(("parallel",)),
        name="rmsnorm",
    )(x, g)


def _mm_kernel(a_ref, w_ref, o_ref):
    o_ref[...] = jnp.dot(a_ref[...], w_ref[...], preferred_element_type=F32).astype(o_ref.dtype)


def matmul(a, w, l, out_dtype, tm=1024, tn=None, name="matmul"):
    T, K = a.shape
    N = w.shape[2]
    tm = _tile(T, tm)
    tn = N if tn is None else _tile(N, tn)
    return pl.pallas_call(
        _mm_kernel,
        grid=(T // tm, N // tn),
        in_specs=[pl.BlockSpec((tm, K), lambda i, j: (i, 0)),
                  pl.BlockSpec((None, K, tn), lambda i, j: (l, 0, j))],
        out_specs=pl.BlockSpec((tm, tn), lambda i, j: (i, j)),
        out_shape=jax.ShapeDtypeStruct((T, N), out_dtype),
        compiler_params=_cparams(("parallel", "parallel")),
        name=name,
    )(a, w)


def _norm_mm_kernel(x_ref, g_ref, w_ref, o_ref):
    h = _rms(x_ref[...], g_ref[...]).astype(BF16)
    o_ref[...] = jnp.dot(h, w_ref[...], preferred_element_type=F32).astype(o_ref.dtype)


def norm_matmul(x, g, w, l, out_dtype, tm=256):
    T, K = x.shape
    N = w.shape[2]
    tm = _tile(T, tm)
    return pl.pallas_call(
        _norm_mm_kernel,
        grid=(T // tm,),
        in_specs=[pl.BlockSpec((tm, K), lambda i: (i, 0)),
                  _vec_spec(K, l, 1),
                  pl.BlockSpec((None, K, N), lambda i: (l, 0, 0))],
        out_specs=pl.BlockSpec((tm, N), lambda i: (i, 0)),
        out_shape=jax.ShapeDtypeStruct((T, N), out_dtype),
        compiler_params=_cparams(("parallel",)),
        name="norm_matmul",
    )(x, g, w)


def _rope_tables(S, scale):
    half = QK_ROPE // 2
    inv = 1.0 / (ROPE_THETA ** (jnp.arange(0, QK_ROPE, 2, dtype=F32) / QK_ROPE))
    ang = jnp.arange(S, dtype=F32)[:, None] * inv[None, :]
    cos, sin = jnp.cos(ang) * scale, jnp.sin(ang) * scale
    z32 = jnp.zeros((S, half), F32)
    z64 = jnp.zeros((S, LANES - QK_ROPE), F32)
    cos_t = jnp.concatenate([cos, cos, z64], axis=1)
    sin_lo = jnp.concatenate([-sin, z32, z64], axis=1)
    sin_hi = jnp.concatenate([z32, sin, z64], axis=1)
    return cos_t, sin_lo, sin_hi


def _rope(v, cos_t, sin_lo, sin_hi):
    half = QK_ROPE // 2
    return (v * cos_t + pltpu.roll(v, half, axis=1) * sin_hi
            + pltpu.roll(v, LANES - half, axis=1) * sin_lo)


def _mla_proj_kernel(scale, nsplit, h_ref, wl_ref, gq_ref, gkv_ref, wq_ref, wkv_ref,
                     qcos_ref, qslo_ref, qshi_ref, kcos_ref, kslo_ref, kshi_ref,
                     q_ref, k_ref, v_ref):
    rows = h_ref.shape[0] // nsplit
    nk = MLA_HEADS * QK_NOPE
    lane = lax.broadcasted_iota(jnp.int32, (rows, LANES), 1)
    ones_col = jnp.where(lane == 0, 1.0, 0.0).astype(v_ref.dtype)
    for p in range(nsplit):
        r = slice(p * rows, (p + 1) * rows)
        z = jnp.dot(h_ref[r, :], wl_ref[...], preferred_element_type=F32)
        cq = _rms(z[:, :Q_LORA], gq_ref[...]).astype(BF16)
        ckv = _rms(z[:, Q_LORA:Q_LORA + KV_LORA], gkv_ref[...]).astype(BF16)
        q = jnp.dot(cq, wq_ref[...], preferred_element_type=F32)
        kv = jnp.dot(ckv, wkv_ref[...], preferred_element_type=F32)
        kr = _rope(z[:, Q_LORA + KV_LORA:], kcos_ref[r, :], kslo_ref[r, :], kshi_ref[r, :])
        kr = kr.astype(k_ref.dtype)
        qcos, qslo, qshi = qcos_ref[r, :], qslo_ref[r, :], qshi_ref[r, :]
        for h in range(MLA_HEADS):
            base = h * QK_PAD
            q_ref[r, base:base + LANES] = (q[:, base:base + LANES] * scale).astype(q_ref.dtype)
            q_ref[r, base + LANES:base + QK_PAD] = _rope(
                q[:, base + LANES:base + QK_PAD], qcos, qslo, qshi).astype(q_ref.dtype)
            k_ref[r, base:base + LANES] = kv[:, h * QK_NOPE:(h + 1) * QK_NOPE].astype(k_ref.dtype)
            k_ref[r, base + LANES:base + QK_PAD] = kr
            v_ref[r, base:base + LANES] = kv[:, nk + h * V_DIM:nk + (h + 1) * V_DIM].astype(v_ref.dtype)
            v_ref[r, base + LANES:base + QK_PAD] = ones_col


def _mla_kernel(tq, tk, nchain, unroll, q_ref, k_ref, v_ref, o_ref, sa_ref, sb_ref):
    S = q_ref.shape[0]
    nq, nk = S // (tq * nchain), S // tk
    nsteps = nq * nk
    assert nsteps % unroll == 0 and unroll % 2 == 0

    def rows(qi, c):
        return pl.ds(pl.multiple_of((qi * nchain + c) * tq, tq), tq)

    def scores(t, dst_ref):
        qi = t // nk
        ki = t - qi * nk
        k = k_ref[pl.ds(pl.multiple_of(ki * tk, tk), tk), :]
        for c in range(nchain):
            dst_ref[c] = lax.dot_general(q_ref[rows(qi, c), :], k, (((1,), (1,)), ((), ())),
                                         preferred_element_type=F32)

    def step(t, cur_ref, nxt_ref, carry):
        t_next = jnp.where(t + 1 == nsteps, 0, t + 1)
        scores(t_next, nxt_ref)
        qi = t // nk
        ki = t - qi * nk
        v = v_ref[pl.ds(pl.multiple_of(ki * tk, tk), tk), :]
        out = []
        for c in range(nchain):
            m, acc = carry[c]
            m = jnp.where(ki == 0, -jnp.inf, m)
            s = cur_ref[c]
            m_new = jnp.maximum(m, jnp.max(s, axis=1, keepdims=True))
            alpha = jnp.exp2(m - m_new)
            p = jnp.exp2(s - m_new).astype(BF16)
            acc = alpha * acc + jnp.dot(p, v, preferred_element_type=F32)
            o_ref[rows(qi, c), :] = (acc[:, :V_DIM] / acc[:, V_DIM:V_DIM + 1]).astype(o_ref.dtype)
            out.append((m_new, acc))
        return tuple(out)

    scores(0, sa_ref)

    def body(i, carry):
        for u in range(unroll):
            cur, nxt = (sa_ref, sb_ref) if u % 2 == 0 else (sb_ref, sa_ref)
            carry = step(unroll * i + u, cur, nxt, carry)
        return carry

    init = tuple((jnp.full((tq, 1), -jnp.inf, F32), jnp.zeros((tq, QK_PAD), F32))
                 for _ in range(nchain))
    lax.fori_loop(0, nsteps // unroll, body, init)


def mla_attention(q, k, v, B, S, tq=256, tk=2048, nchain=2):
    T = B * S
    tq, tk = _tile(S, tq), _tile(S, tk)
    assert S % (tq * nchain) == 0
    nsteps = (S // (tq * nchain)) * (S // tk)
    unroll = next(u for u in (8, 4, 2) if nsteps % u == 0)
    return pl.pallas_call(
        functools.partial(_mla_kernel, tq, tk, nchain, unroll),
        grid=(B, MLA_HEADS),
        in_specs=[pl.BlockSpec((S, QK_PAD), lambda b, h: (b, h)),
                  pl.BlockSpec((S, QK_PAD), lambda b, h: (b, h)),
                  pl.BlockSpec((S, QK_PAD), lambda b, h: (b, h))],
        out_specs=pl.BlockSpec((S, V_DIM), lambda b, h: (b, h)),
        out_shape=jax.ShapeDtypeStruct((T, MLA_HEADS * V_DIM), BF16),
        scratch_shapes=[pltpu.VMEM((nchain, tq, tk), F32), pltpu.VMEM((nchain, tq, tk), F32)],
        compiler_params=_cparams(("parallel", "parallel")),
        name="mla_attention",
    )(q, k, v)


def _window_bias_table():
    G = WIN_HEADS // WIN_KV_HEADS
    qoff = jnp.arange(BLOCK)[:, None]
    col = jnp.arange(3 * BLOCK)[None, :]
    dist = jnp.abs(qoff + BLOCK - col)
    slopes = jnp.asarray([2.0 ** (-8.0 * (h + 1) / WIN_HEADS) for h in range(WIN_HEADS)], F32)
    bias = -slopes[:, None, None] * dist.astype(F32)[None] * LOG2E
    bias = jnp.where((dist <= WINDOW)[None], bias, -jnp.inf)
    return bias.reshape(WIN_KV_HEADS, G * BLOCK, 3 * BLOCK)


def _win_kernel(nb, tq, bias_ref, sink_ref, q_ref, kp_ref, kc_ref, kn_ref,
                vp_ref, vc_ref, vn_ref, o_ref):
    i = pl.program_id(1)
    G = WIN_HEADS // WIN_KV_HEADS
    r = tq // BLOCK
    c_log2 = WIN_HD ** -0.5 * LOG2E
    kb = jnp.concatenate([kp_ref[...], kc_ref[...], kn_ref[...]], axis=0)
    vb = jnp.concatenate([vp_ref[...], vc_ref[...], vn_ref[...]], axis=0)
    col = lax.broadcasted_iota(jnp.int32, (1, 3 * BLOCK), 1)
    gidx = lax.broadcasted_iota(jnp.int32, (G * BLOCK, 1), 0) // BLOCK
    sinks = []
    for kvh in range(WIN_KV_HEADS):
        sk = jnp.zeros((G * BLOCK, 1), F32)
        for g in range(G):
            sk = jnp.where(gidx == g, sink_ref[kvh * G + g] * LOG2E, sk)
        sinks.append(sk)
    for j in range(r):
        gb = i * r + j
        off_seq = ((gb == 0) & (col < BLOCK)) | ((gb == nb - 1) & (col >= 2 * BLOCK))
        edge = jnp.where(off_seq, -jnp.inf, 0.0)
        for kvh in range(WIN_KV_HEADS):
            qs = jnp.concatenate(
                [q_ref[j * BLOCK:(j + 1) * BLOCK, (kvh * G + g) * WIN_HD:(kvh * G + g + 1) * WIN_HD]
                 for g in range(G)], axis=0)
            kk = kb[j * BLOCK:(j + 3) * BLOCK, kvh * WIN_HD:(kvh + 1) * WIN_HD]
            vv = vb[j * BLOCK:(j + 3) * BLOCK, kvh * WIN_HD:(kvh + 1) * WIN_HD]
            s = lax.dot_general(qs, kk, (((1,), (1,)), ((), ())), preferred_element_type=F32)
            s = s * c_log2 + bias_ref[kvh] + edge
            sink = sinks[kvh]
            m = jnp.maximum(jnp.max(s, axis=1, keepdims=True), sink)
            e = jnp.exp2(s - m)
            d = jnp.sum(e, axis=1, keepdims=True) + jnp.exp2(sink - m)
            o = jnp.dot(e.astype(BF16), vv, preferred_element_type=F32) / d
            for g in range(G):
                c0 = (kvh * G + g) * WIN_HD
                o_ref[j * BLOCK:(j + 1) * BLOCK, c0:c0 + WIN_HD] = (
                    o[g * BLOCK:(g + 1) * BLOCK].astype(o_ref.dtype))


def _mem_kernel(q_ref, kv_ref, o_ref):
    scale = MEM_HD ** -0.5
    W = MEM_HEADS * MEM_HD
    for h in range(MEM_HEADS):
        q = q_ref[:, h * MEM_HD:(h + 1) * MEM_HD]
        k = kv_ref[:, h * MEM_HD:(h + 1) * MEM_HD]
        v = kv_ref[:, W + h * MEM_HD:W + (h + 1) * MEM_HD]
        s = lax.dot_general(q, k, (((1,), (1,)), ((), ())), preferred_element_type=F32) * scale
        m = jnp.max(s, axis=1, keepdims=True)
        e = jnp.exp(s - m)
        p = e / jnp.sum(e, axis=1, keepdims=True)
        o_ref[:, h * MEM_HD:(h + 1) * MEM_HD] = jnp.dot(
            p.astype(BF16), v, preferred_element_type=F32).astype(o_ref.dtype)


N_PROJ_IN, N_WIN_IN, N_MEM_IN = 12, 9, 2


def _side_kernel(scale, nsplit, nb, tq, *refs):
    proj_in = refs[:N_PROJ_IN]
    win_in = refs[N_PROJ_IN:N_PROJ_IN + N_WIN_IN]
    mem_in = refs[N_PROJ_IN + N_WIN_IN:N_PROJ_IN + N_WIN_IN + N_MEM_IN]
    q_ref, k_ref, v_ref, win_o_ref, mem_o_ref = refs[N_PROJ_IN + N_WIN_IN + N_MEM_IN:]
    _mla_proj_kernel(scale, nsplit, *proj_in, q_ref, k_ref, v_ref)
    _win_kernel(nb, tq, *win_in, win_o_ref)
    _mem_kernel(*mem_in, mem_o_ref)


def projections_and_local_attention(h, zw, mem_kv, sink, w_lat, g_q, g_kv, w_qm, w_kv,
                                    l, B, S, scale, tq=512, nsplit=2):
    T, D = h.shape
    tq = _tile(S, tq)
    nq = S // tq
    r = tq // BLOCK
    nb = S // BLOCK
    KW = WIN_KV_HEADS * WIN_HD
    WW = WIN_HEADS * WIN_HD
    MW = MEM_HEADS * MEM_HD
    kcol = (WW + MW) // KW
    vcol = kcol + 1
    M = mem_kv.shape[0] // B
    N = MLA_HEADS * QK_PAD
    bias = _window_bias_table()
    q_tabs = _rope_tables(S, scale)
    k_tabs = _rope_tables(S, 1.0)

    tile = lambda w, c=0: pl.BlockSpec((tq, w), lambda b, i: (b * nq + i, c))
    tab = pl.BlockSpec((tq, LANES), lambda b, i: (i, 0))
    full = lambda a: pl.BlockSpec((None,) + a.shape[1:], lambda b, i: (l, 0, 0))
    prev = lambda c: pl.BlockSpec((BLOCK, KW), lambda b, i: (b * nb + jnp.maximum(i * r - 1, 0), c))
    nxt = lambda c: pl.BlockSpec((BLOCK, KW), lambda b, i: (b * nb + jnp.minimum((i + 1) * r, nb - 1), c))

    proj_specs = [tile(D), full(w_lat), _vec_spec(Q_LORA, l, 2), _vec_spec(KV_LORA, l, 2),
                  full(w_qm), full(w_kv), tab, tab, tab, tab, tab, tab]
    win_specs = [pl.BlockSpec(bias.shape, lambda b, i: (0, 0, 0)),
                 pl.BlockSpec(memory_space=pltpu.SMEM),
                 tile(WW), prev(kcol), tile(KW, kcol), nxt(kcol), prev(vcol), tile(KW, vcol), nxt(vcol)]
    mem_specs = [tile(MW, 1), pl.BlockSpec((M, 2 * MW), lambda b, i: (b, 0))]
    assert (len(proj_specs), len(win_specs), len(mem_specs)) == (N_PROJ_IN, N_WIN_IN, N_MEM_IN)
    return pl.pallas_call(
        functools.partial(_side_kernel, scale, nsplit, nb, tq),
        grid=(B, nq),
        in_specs=proj_specs + win_specs + mem_specs,
        out_specs=[tile(N), tile(N), tile(N), tile(WW), tile(MW)],
        out_shape=[jax.ShapeDtypeStruct((T, N), BF16)] * 3 + [
            jax.ShapeDtypeStruct((T, WW), BF16), jax.ShapeDtypeStruct((T, MW), BF16)],
        compiler_params=_cparams(("parallel", "parallel")),
        name="proj_local_attn",
    )(h, w_lat, g_q, g_kv, w_qm, w_kv, *q_tabs, *k_tabs,
      bias, sink.astype(F32), zw, zw, zw, zw, zw, zw, zw,
      zw, mem_kv)


def _merge_kernel(h_ref, a_ref, b_ref, c_ref, wga_ref, wgb_ref, wgc_ref,
                  wa_ref, wb_ref, wc_ref, o_ref):
    h = h_ref[...]

    def branch(x_ref, wg_ref, w_ref):
        gate = jax.nn.sigmoid(jnp.dot(h, wg_ref[...], preferred_element_type=F32))
        return gate * jnp.dot(x_ref[...], w_ref[...], preferred_element_type=F32)

    o_ref[...] = (branch(a_ref, wga_ref, wa_ref) + branch(b_ref, wgb_ref, wb_ref)
                  + branch(c_ref, wgc_ref, wc_ref)).astype(o_ref.dtype)


def gated_merge(h, a, b, c, w_g, w_a, w_b, w_c, l, tm=1024, tn=MERGE_TN):
    T, D = h.shape
    tm, tn = _tile(T, tm), _tile(D, tn)
    nj = D // tn
    act = lambda w: pl.BlockSpec((tm, w), lambda i, j: (i, 0))
    wg = lambda br: pl.BlockSpec((None, D, tn), lambda i, j: (l, 0, br * nj + j))
    wx = lambda k: pl.BlockSpec((None, k, tn), lambda i, j: (l, 0, j))
    return pl.pallas_call(
        _merge_kernel,
        grid=(T // tm, nj),
        in_specs=[act(D), act(a.shape[1]), act(b.shape[1]), act(c.shape[1]),
                  wg(0), wg(1), wg(2),
                  wx(a.shape[1]), wx(b.shape[1]), wx(c.shape[1])],
        out_specs=pl.BlockSpec((tm, tn), lambda i, j: (i, j)),
        out_shape=jax.ShapeDtypeStruct((T, D), BF16),
        compiler_params=_cparams(("parallel", "parallel")),
        name="gated_merge",
    )(h, a, b, c, w_g, w_g, w_g, w_a, w_b, w_c)


def _outproj_kernel(nsplit, m_ref, w_ref, x_ref, gpost_ref, gnext_ref, xo_ref, ho_ref):
    rows = m_ref.shape[0] // nsplit
    for p in range(nsplit):
        r = slice(p * rows, (p + 1) * rows)
        y = jnp.dot(m_ref[r, :], w_ref[...], preferred_element_type=F32)
        x_new = x_ref[r, :] + _rms(y, gpost_ref[...])
        xo_ref[r, :] = x_new
        ho_ref[r, :] = _rms(x_new, gnext_ref[...]).astype(ho_ref.dtype)


def out_proj(merged, w_out, x, g_post, g_next, l, tm=512, nsplit=2):
    T, D = x.shape
    tm = _tile(T, tm)
    row = lambda: pl.BlockSpec((tm, D), lambda i: (i, 0))
    return pl.pallas_call(
        functools.partial(_outproj_kernel, nsplit),
        grid=(T // tm,),
        in_specs=[row(), pl.BlockSpec((None, D, D), lambda i: (l, 0, 0)), row(),
                  _vec_spec(D, l, 1), _vec_spec(D, l, 1)],
        out_specs=[row(), row()],
        out_shape=[jax.ShapeDtypeStruct((T, D), F32), jax.ShapeDtypeStruct((T, D), BF16)],
        compiler_params=_cparams(("parallel",)),
        name="out_proj",
    )(merged, w_out, x, g_post, g_next)


def _ffn_kernel(emit_h, nsplit, h_ref, wu_ref, wd_ref, x_ref, gpost_ref, gnext_ref, xo_ref, *rest):
    ho_ref, acc_ref = rest if emit_h else (None, rest[0])
    k = pl.program_id(1)
    last = pl.num_programs(1) - 1

    def partial_sum(r):
        u = jnp.dot(h_ref[r, :], wu_ref[...], preferred_element_type=F32)
        u = jnp.square(jnp.maximum(u, 0.0)).astype(BF16)
        return jnp.dot(u, wd_ref[...], preferred_element_type=F32)

    @pl.when(k == 0)
    def _():
        acc_ref[...] = jnp.zeros_like(acc_ref)

    @pl.when(k < last)
    def _():
        acc_ref[...] += partial_sum(slice(None))

    @pl.when(k == last)
    def _():
        rows = h_ref.shape[0] // nsplit
        for p in range(nsplit):
            r = slice(p * rows, (p + 1) * rows)
            y = acc_ref[r, :] + partial_sum(r)
            x_new = x_ref[r, :] + _rms(y, gpost_ref[...])
            xo_ref[r, :] = x_new
            if emit_h:
                ho_ref[r, :] = _rms(x_new, gnext_ref[...]).astype(ho_ref.dtype)


def ffn(h, w_up, w_down, x, g_post, g_next, l, l_next, tm=512, tf=FFN_TF, nsplit=2):
    T, D = x.shape
    Fd = w_up.shape[2]
    tm, tf = _tile(T, tm), _tile(Fd, tf)
    nf = Fd // tf
    emit_h = l_next is not None
    row = lambda: pl.BlockSpec((tm, D), lambda i, k: (i, 0))
    out_specs = [row()] + ([row()] if emit_h else [])
    out_shape = [jax.ShapeDtypeStruct((T, D), F32)] + (
        [jax.ShapeDtypeStruct((T, D), BF16)] if emit_h else [])
    res = pl.pallas_call(
        functools.partial(_ffn_kernel, emit_h, nsplit),
        grid=(T // tm, nf),
        in_specs=[row(),
                  pl.BlockSpec((None, D, tf), lambda i, k: (l, 0, k)),
                  pl.BlockSpec((None, tf, D), lambda i, k: (l, k, 0)),
                  row(), _vec_spec(D, l, 2), _vec_spec(D, l_next if emit_h else l, 2)],
        out_specs=out_specs,
        out_shape=out_shape,
        scratch_shapes=[pltpu.VMEM((tm, D), F32)],
        compiler_params=_cparams(("parallel", "arbitrary")),
        name="ffn",
    )(h, w_up, w_down, x, g_post, g_next)
    return (res[0], res[1]) if emit_h else (res[0], None)


def _prep_in_proj(w_in):
    L, D, _ = w_in.shape
    o_lat = Q_LORA + KV_LORA + QK_ROPE
    o_qw = o_lat + WIN_HEADS * WIN_HD
    o_kw = o_qw + WIN_KV_HEADS * WIN_HD
    o_vw = o_kw + WIN_KV_HEADS * WIN_HD
    o_qm = o_vw + MEM_HEADS * MEM_HD
    w = w_in.astype(BF16)
    w_lat = jnp.concatenate([w[:, :, :o_lat], jnp.zeros((L, D, LAT_W - o_lat), BF16)], axis=2)
    w_zw = jnp.concatenate([w[:, :, o_lat:o_qw], w[:, :, o_vw:o_qm],
                            w[:, :, o_qw:o_kw], w[:, :, o_kw:o_vw]], axis=2)
    w_g = w[:, :, o_qm:]
    return w_lat, w_zw, w_g


def _prep_mla(w_uq, w_ukv):
    L = w_uq.shape[0]
    wq = w_uq.astype(BF16).reshape(L, Q_LORA, MLA_HEADS, QK_NOPE + QK_ROPE)
    wq = jnp.pad(wq, ((0, 0), (0, 0), (0, 0), (0, QK_PAD - QK_NOPE - QK_ROPE)))
    wq = wq.reshape(L, Q_LORA, MLA_HEADS * QK_PAD)
    wkv = w_ukv.astype(BF16).reshape(L, KV_LORA, MLA_HEADS, QK_NOPE + V_DIM)
    wkv = jnp.concatenate([wkv[..., :QK_NOPE].reshape(L, KV_LORA, -1),
                           wkv[..., QK_NOPE:].reshape(L, KV_LORA, -1)], axis=2)
    return wq, wkv


def kernel(x_prompt, x_sample, mem_prompt, mem_sample, g_attn_pre, g_attn_post, w_in, g_q_lat, g_kv_lat, w_uq, w_ukv, g_mem, w_mem_kv, sink_logit, w_branch_mla, w_branch_win, w_branch_mem, w_out, g_ffn_pre, g_ffn_post, w_ff_up, w_ff_down):
    depth = w_in.shape[0]
    scale = (QK_NOPE + QK_ROPE) ** -0.5 * LOG2E

    w_lat, w_zw, w_g = _prep_in_proj(w_in)
    wq, wkv = _prep_mla(w_uq, w_ukv)
    wmem = w_mem_kv.astype(BF16)
    wa = w_branch_mla.astype(BF16)
    wb = w_branch_win.astype(BF16)
    wc = w_branch_mem.astype(BF16)
    wo = w_out.astype(BF16)
    wup = w_ff_up.astype(BF16)
    wdn = w_ff_down.astype(BF16)
    vec = lambda g: g.reshape(depth, 1, g.shape[1])
    g_pre, g_post, g_q, g_kv = vec(g_attn_pre), vec(g_attn_post), vec(g_q_lat), vec(g_kv_lat)
    g_m, g_fpre, g_fpost = vec(g_mem), vec(g_ffn_pre), vec(g_ffn_post)

    def run(x3, mem3):
        B, S, D = x3.shape
        x = x3.reshape(B * S, D)
        mem = mem3.reshape(B * mem3.shape[1], D)
        h = rmsnorm_bf16(x, g_pre, 0)
        for l in range(depth):
            zw = matmul(h, w_zw, l, BF16, tn=w_zw.shape[2] // 2, name="in_proj_qkv")
            mem_kv = norm_matmul(mem, g_m, wmem, l, BF16)
            q, k, v, b, c = projections_and_local_attention(
                h, zw, mem_kv, sink_logit[l], w_lat, g_q, g_kv, wq, wkv, l, B, S, scale)
            a = mla_attention(q, k, v, B, S)
            merged = gated_merge(h, a, b, c, w_g, wa, wb, wc, l)
            x, h = out_proj(merged, wo, x, g_post, g_fpre, l)
            x, h = ffn(h, wup, wdn, x, g_fpost, g_pre, l, l + 1 if l + 1 < depth else None)
        return x.reshape(B, S, D)

    return (run(x_prompt, mem_prompt), run(x_sample, mem_sample))
```

```python
import functools
import math

import jax
import jax.numpy as jnp
from jax import lax
from jax.experimental import pallas as pl
from jax.experimental.pallas import tpu as pltpu

F32 = jnp.float32
BF16 = jnp.bfloat16

EPS = 1e-6
ROPE_THETA = 10000.0
LOG2E = math.log2(math.e)

MLA_HEADS = 8
QK_NOPE = 128
QK_ROPE = 64
V_DIM = 128
Q_LORA = 512
KV_LORA = 512
WIN_HEADS = 8
WIN_KV_HEADS = 2
WIN_HD = 128
WINDOW = 128
BLOCK = 128
MEM_HEADS = 4
MEM_HD = 256
N_BRANCH = 3

LANES = 128
QK_PAD = 2 * LANES
LAT_W = Q_LORA + KV_LORA + LANES

MERGE_TN = 512
FFN_TF = 1024
VMEM_LIMIT = 56 * 1024 * 1024


def _cparams(sem):
    return pltpu.CompilerParams(dimension_semantics=sem, vmem_limit_bytes=VMEM_LIMIT)


def _rms(x, g):
    r = lax.rsqrt(jnp.mean(x * x, axis=-1, keepdims=True) + EPS)
    return (x * r) * g


def _tile(n, pref):
    t = min(n, pref)
    assert n % t == 0, (n, t)
    return t


def _vec_spec(n, l, nargs):
    if nargs == 1:
        return pl.BlockSpec((None, 1, n), lambda i: (l, 0, 0))
    return pl.BlockSpec((None, 1, n), lambda i, j: (l, 0, 0))


def _norm_kernel(x_ref, g_ref, o_ref):
    o_ref[...] = _rms(x_ref[...], g_ref[...]).astype(o_ref.dtype)


def rmsnorm_bf16(x, g, l):
    T, D = x.shape
    tm = _tile(T, 512)
    return pl.pallas_call(
        _norm_kernel,
        grid=(T // tm,),
        in_specs=[pl.BlockSpec((tm, D), lambda i: (i, 0)), _vec_spec(D, l, 1)],
        out_specs=pl.BlockSpec((tm, D), lambda i: (i, 0)),
        out_shape=jax.ShapeDtypeStruct((T, D), BF16),
        compiler_params=_cparams(("parallel",)),
        name="rmsnorm",
    )(x, g)


def _mm_kernel(a_ref, w_ref, o_ref):
    o_ref[...] = jnp.dot(a_ref[...], w_ref[...], preferred_element_type=F32).astype(o_ref.dtype)


def matmul(a, w, l, out_dtype, tm=1024, tn=None, name="matmul"):
    T, K = a.shape
    N = w.shape[2]
    tm = _tile(T, tm)
    tn = N if tn is None else _tile(N, tn)
    return pl.pallas_call(
        _mm_kernel,
        grid=(T // tm, N // tn),
        in_specs=[pl.BlockSpec((tm, K), lambda i, j: (i, 0)),
                  pl.BlockSpec((None, K, tn), lambda i, j: (l, 0, j))],
        out_specs=pl.BlockSpec((tm, tn), lambda i, j: (i, j)),
        out_shape=jax.ShapeDtypeStruct((T, N), out_dtype),
        compiler_params=_cparams(("parallel", "parallel")),
        name=name,
    )(a, w)


def _norm_mm_kernel(x_ref, g_ref, w_ref, o_ref):
    h = _rms(x_ref[...], g_ref[...]).astype(BF16)
    o_ref[...] = jnp.dot(h, w_ref[...], preferred_element_type=F32).astype(o_ref.dtype)


def norm_matmul(x, g, w, l, out_dtype, tm=256):
    T, K = x.shape
    N = w.shape[2]
    tm = _tile(T, tm)
    return pl.pallas_call(
        _norm_mm_kernel,
        grid=(T // tm,),
        in_specs=[pl.BlockSpec((tm, K), lambda i: (i, 0)),
                  _vec_spec(K, l, 1),
                  pl.BlockSpec((None, K, N), lambda i: (l, 0, 0))],
        out_specs=pl.BlockSpec((tm, N), lambda i: (i, 0)),
        out_shape=jax.ShapeDtypeStruct((T, N), out_dtype),
        compiler_params=_cparams(("parallel",)),
        name="norm_matmul",
    )(x, g, w)


def _rope_tables(S, scale):
    half = QK_ROPE // 2
    inv = 1.0 / (ROPE_THETA ** (jnp.arange(0, QK_ROPE, 2, dtype=F32) / QK_ROPE))
    ang = jnp.arange(S, dtype=F32)[:, None] * inv[None, :]
    cos, sin = jnp.cos(ang) * scale, jnp.sin(ang) * scale
    z32 = jnp.zeros((S, half), F32)
    z64 = jnp.zeros((S, LANES - QK_ROPE), F32)
    cos_t = jnp.concatenate([cos, cos, z64], axis=1)
    sin_lo = jnp.concatenate([-sin, z32, z64], axis=1)
    sin_hi = jnp.concatenate([z32, sin, z64], axis=1)
    return cos_t, sin_lo, sin_hi


def _rope(v, cos_t, sin_lo, sin_hi):
    half = QK_ROPE // 2
    return (v * cos_t + pltpu.roll(v, half, axis=1) * sin_hi
            + pltpu.roll(v, LANES - half, axis=1) * sin_lo)


def _mla_proj_kernel(scale, nsplit, h_ref, wl_ref, gq_ref, gkv_ref, wq_ref, wkv_ref,
                     qcos_ref, qslo_ref, qshi_ref, kcos_ref, kslo_ref, kshi_ref,
                     q_ref, k_ref, v_ref):
    rows = h_ref.shape[0] // nsplit
    nk = MLA_HEADS * QK_NOPE
    lane = lax.broadcasted_iota(jnp.int32, (rows, LANES), 1)
    ones_col = jnp.where(lane == 0, 1.0, 0.0).astype(v_ref.dtype)
    for p in range(nsplit):
        r = slice(p * rows, (p + 1) * rows)
        z = jnp.dot(h_ref[r, :], wl_ref[...], preferred_element_type=F32)
        cq = _rms(z[:, :Q_LORA], gq_ref[...]).astype(BF16)
        ckv = _rms(z[:, Q_LORA:Q_LORA + KV_LORA], gkv_ref[...]).astype(BF16)
        q = jnp.dot(cq, wq_ref[...], preferred_element_type=F32)
        kv = jnp.dot(ckv, wkv_ref[...], preferred_element_type=F32)
        kr = _rope(z[:, Q_LORA + KV_LORA:], kcos_ref[r, :], kslo_ref[r, :], kshi_ref[r, :])
        kr = kr.astype(k_ref.dtype)
        qcos, qslo, qshi = qcos_ref[r, :], qslo_ref[r, :], qshi_ref[r, :]
        for h in range(MLA_HEADS):
            base = h * QK_PAD
            q_ref[r, base:base + LANES] = (q[:, base:base + LANES] * scale).astype(q_ref.dtype)
            q_ref[r, base + LANES:base + QK_PAD] = _rope(
                q[:, base + LANES:base + QK_PAD], qcos, qslo, qshi).astype(q_ref.dtype)
            k_ref[r, base:base + LANES] = kv[:, h * QK_NOPE:(h + 1) * QK_NOPE].astype(k_ref.dtype)
            k_ref[r, base + LANES:base + QK_PAD] = kr
            v_ref[r, base:base + LANES] = kv[:, nk + h * V_DIM:nk + (h + 1) * V_DIM].astype(v_ref.dtype)
            v_ref[r, base + LANES:base + QK_PAD] = ones_col


def _mla_kernel(tq, tk, nchain, unroll, q_ref, k_ref, v_ref, o_ref, sa_ref, sb_ref):
    S = q_ref.shape[0]
    nq, nk = S // (tq * nchain), S // tk
    nsteps = nq * nk
    assert nsteps % unroll == 0 and unroll % 2 == 0

    def rows(qi, c):
        return pl.ds(pl.multiple_of((qi * nchain + c) * tq, tq), tq)

    def scores(t, dst_ref):
        qi = t // nk
        ki = t - qi * nk
        k = k_ref[pl.ds(pl.multiple_of(ki * tk, tk), tk), :]
        for c in range(nchain):
            dst_ref[c] = lax.dot_general(q_ref[rows(qi, c), :], k, (((1,), (1,)), ((), ())),
                                         preferred_element_type=F32)

    def step(t, cur_ref, nxt_ref, carry):
        t_next = jnp.where(t + 1 == nsteps, 0, t + 1)
        scores(t_next, nxt_ref)
        qi = t // nk
        ki = t - qi * nk
        v = v_ref[pl.ds(pl.multiple_of(ki * tk, tk), tk), :]
        out = []
        for c in range(nchain):
            m, acc = carry[c]
            m = jnp.where(ki == 0, -jnp.inf, m)
            s = cur_ref[c]
            m_new = jnp.maximum(m, jnp.max(s, axis=1, keepdims=True))
            alpha = jnp.exp2(m - m_new)
            p = jnp.exp2(s - m_new).astype(BF16)
            acc = alpha * acc + jnp.dot(p, v, preferred_element_type=F32)
            o_ref[rows(qi, c), :] = (acc[:, :V_DIM] / acc[:, V_DIM:V_DIM + 1]).astype(o_ref.dtype)
            out.append((m_new, acc))
        return tuple(out)

    scores(0, sa_ref)

    def body(i, carry):
        for u in range(unroll):
            cur, nxt = (sa_ref, sb_ref) if u % 2 == 0 else (sb_ref, sa_ref)
            carry = step(unroll * i + u, cur, nxt, carry)
        return carry

    init = tuple((jnp.full((tq, 1), -jnp.inf, F32), jnp.zeros((tq, QK_PAD), F32))
                 for _ in range(nchain))
    lax.fori_loop(0, nsteps // unroll, body, init)


def mla_attention(q, k, v, B, S, tq=256, tk=2048, nchain=2):
    T = B * S
    tq, tk = _tile(S, tq), _tile(S, tk)
    assert S % (tq * nchain) == 0
    nsteps = (S // (tq * nchain)) * (S // tk)
    unroll = next(u for u in (8, 4, 2) if nsteps % u == 0)
    return pl.pallas_call(
        functools.partial(_mla_kernel, tq, tk, nchain, unroll),
        grid=(B, MLA_HEADS),
        in_specs=[pl.BlockSpec((S, QK_PAD), lambda b, h: (b, h)),
                  pl.BlockSpec((S, QK_PAD), lambda b, h: (b, h)),
                  pl.BlockSpec((S, QK_PAD), lambda b, h: (b, h))],
        out_specs=pl.BlockSpec((S, V_DIM), lambda b, h: (b, h)),
        out_shape=jax.ShapeDtypeStruct((T, MLA_HEADS * V_DIM), BF16),
        scratch_shapes=[pltpu.VMEM((nchain, tq, tk), F32), pltpu.VMEM((nchain, tq, tk), F32)],
        compiler_params=_cparams(("parallel", "parallel")),
        name="mla_attention",
    )(q, k, v)


def _window_bias_table():
    G = WIN_HEADS // WIN_KV_HEADS
    qoff = jnp.arange(BLOCK)[:, None]
    col = jnp.arange(3 * BLOCK)[None, :]
    dist = jnp.abs(qoff + BLOCK - col)
    slopes = jnp.asarray([2.0 ** (-8.0 * (h + 1) / WIN_HEADS) for h in range(WIN_HEADS)], F32)
    bias = -slopes[:, None, None] * dist.astype(F32)[None] * LOG2E
    bias = jnp.where((dist <= WINDOW)[None], bias, -jnp.inf)
    return bias.reshape(WIN_KV_HEADS, G * BLOCK, 3 * BLOCK)


def _win_kernel(nb, tq, bias_ref, sink_ref, q_ref, kp_ref, kc_ref, kn_ref,
                vp_ref, vc_ref, vn_ref, o_ref):
    i = pl.program_id(1)
    G = WIN_HEADS // WIN_KV_HEADS
    r = tq // BLOCK
    kb = jnp.concatenate([kp_ref[...], kc_ref[...], kn_ref[...]], axis=0)
    vb = jnp.concatenate([vp_ref[...], vc_ref[...], vn_ref[...]], axis=0)
    col = lax.broadcasted_iota(jnp.int32, (1, 3 * BLOCK), 1)
    gidx = lax.broadcasted_iota(jnp.int32, (G * BLOCK, 1), 0) // BLOCK
    sinks = []
    for kvh in range(WIN_KV_HEADS):
        sk = jnp.zeros((G * BLOCK, 1), F32)
        for g in range(G):
            sk = jnp.where(gidx == g, sink_ref[kvh * G + g] * LOG2E, sk)
        sinks.append(sk)
    lane = lax.broadcasted_iota(jnp.int32, (3 * BLOCK, WIN_HD), 1)
    ones_col = jnp.where(lane == 0, 1.0, 0.0).astype(vb.dtype)
    for j in range(r):
        edge = None
        if j == 0:
            edge = jnp.where((i == 0) & (col < BLOCK), -jnp.inf, 0.0)
        if j == r - 1:
            hi = jnp.where((i * r + j == nb - 1) & (col >= 2 * BLOCK), -jnp.inf, 0.0)
            edge = hi if edge is None else edge + hi
        for kvh in range(WIN_KV_HEADS):
            qs = jnp.concatenate(
                [q_ref[j * BLOCK:(j + 1) * BLOCK, (kvh * G + g) * WIN_HD:(kvh * G + g + 1) * WIN_HD]
                 for g in range(G)], axis=0)
            kk = kb[j * BLOCK:(j + 3) * BLOCK, kvh * WIN_HD:(kvh + 1) * WIN_HD]
            vv = jnp.concatenate(
                [vb[j * BLOCK:(j + 3) * BLOCK, kvh * WIN_HD:(kvh + 1) * WIN_HD], ones_col], axis=1)
            s = lax.dot_general(qs, kk, (((1,), (1,)), ((), ())), preferred_element_type=F32)
            s = s + bias_ref[kvh]
            if edge is not None:
                s = s + edge
            sink = sinks[kvh]
            m = jnp.maximum(jnp.max(s, axis=1, keepdims=True), sink)
            ev = jnp.dot(jnp.exp2(s - m).astype(BF16), vv, preferred_element_type=F32)
            o = ev[:, :WIN_HD] / (ev[:, WIN_HD:WIN_HD + 1] + jnp.exp2(sink - m))
            for g in range(G):
                c0 = (kvh * G + g) * WIN_HD
                o_ref[j * BLOCK:(j + 1) * BLOCK, c0:c0 + WIN_HD] = (
                    o[g * BLOCK:(g + 1) * BLOCK].astype(o_ref.dtype))


def _mem_kernel(q_ref, kv_ref, o_ref):
    W = MEM_HEADS * MEM_HD
    for h in range(MEM_HEADS):
        q = q_ref[:, h * MEM_HD:(h + 1) * MEM_HD]
        k = kv_ref[:, h * MEM_HD:(h + 1) * MEM_HD]
        v = kv_ref[:, W + h * MEM_HD:W + (h + 1) * MEM_HD]
        s = lax.dot_general(q, k, (((1,), (1,)), ((), ())), preferred_element_type=F32)
        m = jnp.max(s, axis=1, keepdims=True)
        e = jnp.exp2(s - m)
        p = e / jnp.sum(e, axis=1, keepdims=True)
        o_ref[:, h * MEM_HD:(h + 1) * MEM_HD] = jnp.dot(
            p.astype(BF16), v, preferred_element_type=F32).astype(o_ref.dtype)


N_PROJ_IN, N_WIN_IN, N_MEM_IN = 12, 9, 2


def _side_kernel(scale, nsplit, nb, tq, *refs):
    proj_in = refs[:N_PROJ_IN]
    win_in = refs[N_PROJ_IN:N_PROJ_IN + N_WIN_IN]
    mem_in = refs[N_PROJ_IN + N_WIN_IN:N_PROJ_IN + N_WIN_IN + N_MEM_IN]
    q_ref, k_ref, v_ref, win_o_ref, mem_o_ref = refs[N_PROJ_IN + N_WIN_IN + N_MEM_IN:]
    _mla_proj_kernel(scale, nsplit, *proj_in, q_ref, k_ref, v_ref)
    _win_kernel(nb, tq, *win_in, win_o_ref)
    _mem_kernel(*mem_in, mem_o_ref)


def projections_and_local_attention(h, zw, mem_kv, sink, w_lat, g_q, g_kv, w_qm, w_kv,
                                    l, B, S, scale, tq=512, nsplit=2):
    T, D = h.shape
    tq = _tile(S, tq)
    nq = S // tq
    r = tq // BLOCK
    nb = S // BLOCK
    KW = WIN_KV_HEADS * WIN_HD
    WW = WIN_HEADS * WIN_HD
    MW = MEM_HEADS * MEM_HD
    kcol = (WW + MW) // KW
    vcol = kcol + 1
    M = mem_kv.shape[0] // B
    N = MLA_HEADS * QK_PAD
    bias = _window_bias_table()
    q_tabs = _rope_tables(S, scale)
    k_tabs = _rope_tables(S, 1.0)

    tile = lambda w, c=0: pl.BlockSpec((tq, w), lambda b, i: (b * nq + i, c))
    tab = pl.BlockSpec((tq, LANES), lambda b, i: (i, 0))
    full = lambda a: pl.BlockSpec((None,) + a.shape[1:], lambda b, i: (l, 0, 0))
    prev = lambda c: pl.BlockSpec((BLOCK, KW), lambda b, i: (b * nb + jnp.maximum(i * r - 1, 0), c))
    nxt = lambda c: pl.BlockSpec((BLOCK, KW), lambda b, i: (b * nb + jnp.minimum((i + 1) * r, nb - 1), c))

    proj_specs = [tile(D), full(w_lat), _vec_spec(Q_LORA, l, 2), _vec_spec(KV_LORA, l, 2),
                  full(w_qm), full(w_kv), tab, tab, tab, tab, tab, tab]
    win_specs = [pl.BlockSpec(bias.shape, lambda b, i: (0, 0, 0)),
                 pl.BlockSpec(memory_space=pltpu.SMEM),
                 tile(WW), prev(kcol), tile(KW, kcol), nxt(kcol), prev(vcol), tile(KW, vcol), nxt(vcol)]
    mem_specs = [tile(MW, 1), pl.BlockSpec((M, 2 * MW), lambda b, i: (b, 0))]
    assert (len(proj_specs), len(win_specs), len(mem_specs)) == (N_PROJ_IN, N_WIN_IN, N_MEM_IN)
    return pl.pallas_call(
        functools.partial(_side_kernel, scale, nsplit, nb, tq),
        grid=(B, nq),
        in_specs=proj_specs + win_specs + mem_specs,
        out_specs=[tile(N), tile(N), tile(N), tile(WW), tile(MW)],
        out_shape=[jax.ShapeDtypeStruct((T, N), BF16)] * 3 + [
            jax.ShapeDtypeStruct((T, WW), BF16), jax.ShapeDtypeStruct((T, MW), BF16)],
        compiler_params=_cparams(("parallel", "parallel")),
        name="proj_local_attn",
    )(h, w_lat, g_q, g_kv, w_qm, w_kv, *q_tabs, *k_tabs,
      bias, sink.astype(F32), zw, zw, zw, zw, zw, zw, zw,
      zw, mem_kv)


def _merge_kernel(h_ref, a_ref, b_ref, c_ref, wga_ref, wgb_ref, wgc_ref,
                  wa_ref, wb_ref, wc_ref, o_ref):
    h = h_ref[...]

    def branch(x_ref, wg_ref, w_ref):
        gate = jax.nn.sigmoid(jnp.dot(h, wg_ref[...], preferred_element_type=F32))
        return gate * jnp.dot(x_ref[...], w_ref[...], preferred_element_type=F32)

    o_ref[...] = (branch(a_ref, wga_ref, wa_ref) + branch(b_ref, wgb_ref, wb_ref)
                  + branch(c_ref, wgc_ref, wc_ref)).astype(o_ref.dtype)


def gated_merge(h, a, b, c, w_g, w_a, w_b, w_c, l, tm=1024, tn=MERGE_TN):
    T, D = h.shape
    tm, tn = _tile(T, tm), _tile(D, tn)
    nj = D // tn
    act = lambda w: pl.BlockSpec((tm, w), lambda i, j: (i, 0))
    wg = lambda br: pl.BlockSpec((None, D, tn), lambda i, j: (l, 0, br * nj + j))
    wx = lambda k: pl.BlockSpec((None, k, tn), lambda i, j: (l, 0, j))
    return pl.pallas_call(
        _merge_kernel,
        grid=(T // tm, nj),
        in_specs=[act(D), act(a.shape[1]), act(b.shape[1]), act(c.shape[1]),
                  wg(0), wg(1), wg(2),
                  wx(a.shape[1]), wx(b.shape[1]), wx(c.shape[1])],
        out_specs=pl.BlockSpec((tm, tn), lambda i, j: (i, j)),
        out_shape=jax.ShapeDtypeStruct((T, D), BF16),
        compiler_params=_cparams(("parallel", "parallel")),
        name="gated_merge",
    )(h, a, b, c, w_g, w_g, w_g, w_a, w_b, w_c)


def _outproj_kernel(nsplit, m_ref, w_ref, x_ref, gpost_ref, gnext_ref, xo_ref, ho_ref):
    rows = m_ref.shape[0] // nsplit
    for p in range(nsplit):
        r = slice(p * rows, (p + 1) * rows)
        y = jnp.dot(m_ref[r, :], w_ref[...], preferred_element_type=F32)
        x_new = x_ref[r, :] + _rms(y, gpost_ref[...])
        xo_ref[r, :] = x_new
        ho_ref[r, :] = _rms(x_new, gnext_ref[...]).astype(ho_ref.dtype)


def out_proj(merged, w_out, x, g_post, g_next, l, tm=512, nsplit=4):
    T, D = x.shape
    tm = _tile(T, tm)
    row = lambda: pl.BlockSpec((tm, D), lambda i: (i, 0))
    return pl.pallas_call(
        functools.partial(_outproj_kernel, nsplit),
        grid=(T // tm,),
        in_specs=[row(), pl.BlockSpec((None, D, D), lambda i: (l, 0, 0)), row(),
                  _vec_spec(D, l, 1), _vec_spec(D, l, 1)],
        out_specs=[row(), row()],
        out_shape=[jax.ShapeDtypeStruct((T, D), F32), jax.ShapeDtypeStruct((T, D), BF16)],
        compiler_params=_cparams(("parallel",)),
        name="out_proj",
    )(merged, w_out, x, g_post, g_next)


def _ffn_kernel(emit_h, nsplit, nf, h_ref, wu_ref, wd_ref, x_ref, gpost_ref, gnext_ref, xo_ref, *rest):
    ho_ref, acc_ref = rest if emit_h else (None, rest[0])
    k = pl.program_id(1)
    last = pl.num_programs(1) - 1

    def partial_sum(r):
        u = jnp.dot(h_ref[r, :], wu_ref[...], preferred_element_type=F32)
        u = jnp.square(jnp.maximum(u, 0.0)).astype(BF16)
        return jnp.dot(u, wd_ref[...], preferred_element_type=F32)

    if nf > 1:
        @pl.when(k == 0)
        def _():
            acc_ref[...] = partial_sum(slice(None))

        @pl.when((k > 0) & (k < last))
        def _():
            acc_ref[...] += partial_sum(slice(None))

    @pl.when(k == last)
    def _():
        rows = h_ref.shape[0] // nsplit
        for p in range(nsplit):
            r = slice(p * rows, (p + 1) * rows)
            y = partial_sum(r)
            if nf > 1:
                y = acc_ref[r, :] + y
            x_new = x_ref[r, :] + _rms(y, gpost_ref[...])
            xo_ref[r, :] = x_new
            if emit_h:
                ho_ref[r, :] = _rms(x_new, gnext_ref[...]).astype(ho_ref.dtype)


def ffn(h, w_up, w_down, x, g_post, g_next, l, l_next, tm=512, tf=FFN_TF, nsplit=2):
    T, D = x.shape
    Fd = w_up.shape[2]
    tm, tf = _tile(T, tm), _tile(Fd, tf)
    nf = Fd // tf
    emit_h = l_next is not None
    row = lambda: pl.BlockSpec((tm, D), lambda i, k: (i, 0))
    out_specs = [row()] + ([row()] if emit_h else [])
    out_shape = [jax.ShapeDtypeStruct((T, D), F32)] + (
        [jax.ShapeDtypeStruct((T, D), BF16)] if emit_h else [])
    res = pl.pallas_call(
        functools.partial(_ffn_kernel, emit_h, nsplit, nf),
        grid=(T // tm, nf),
        in_specs=[row(),
                  pl.BlockSpec((None, D, tf), lambda i, k: (l, 0, k)),
                  pl.BlockSpec((None, tf, D), lambda i, k: (l, k, 0)),
                  row(), _vec_spec(D, l, 2), _vec_spec(D, l_next if emit_h else l, 2)],
        out_specs=out_specs,
        out_shape=out_shape,
        scratch_shapes=[pltpu.VMEM((tm, D), F32)],
        compiler_params=_cparams(("parallel", "arbitrary")),
        name="ffn",
    )(h, w_up, w_down, x, g_post, g_next)
    return (res[0], res[1]) if emit_h else (res[0], None)


def _prep_in_proj(w_in):
    L, D, _ = w_in.shape
    o_lat = Q_LORA + KV_LORA + QK_ROPE
    o_qw = o_lat + WIN_HEADS * WIN_HD
    o_kw = o_qw + WIN_KV_HEADS * WIN_HD
    o_vw = o_kw + WIN_KV_HEADS * WIN_HD
    o_qm = o_vw + MEM_HEADS * MEM_HD
    w = w_in
    w_lat = jnp.concatenate([w[:, :, :o_lat], jnp.zeros((L, D, LAT_W - o_lat), w.dtype)], axis=2)
    w_zw = jnp.concatenate([w[:, :, o_lat:o_qw] * (WIN_HD ** -0.5 * LOG2E),
                            w[:, :, o_vw:o_qm] * (MEM_HD ** -0.5 * LOG2E),
                            w[:, :, o_qw:o_kw], w[:, :, o_kw:o_vw]], axis=2)
    w_g = w[:, :, o_qm:]
    return w_lat.astype(BF16), w_zw.astype(BF16), w_g.astype(BF16)


def _prep_mla(w_uq, w_ukv):
    L = w_uq.shape[0]
    wq = w_uq.astype(BF16).reshape(L, Q_LORA, MLA_HEADS, QK_NOPE + QK_ROPE)
    wq = jnp.pad(wq, ((0, 0), (0, 0), (0, 0), (0, QK_PAD - QK_NOPE - QK_ROPE)))
    wq = wq.reshape(L, Q_LORA, MLA_HEADS * QK_PAD)
    wkv = w_ukv.astype(BF16).reshape(L, KV_LORA, MLA_HEADS, QK_NOPE + V_DIM)
    wkv = jnp.concatenate([wkv[..., :QK_NOPE].reshape(L, KV_LORA, -1),
                           wkv[..., QK_NOPE:].reshape(L, KV_LORA, -1)], axis=2)
    return wq, wkv


def kernel(x_prompt, x_sample, mem_prompt, mem_sample, g_attn_pre, g_attn_post, w_in, g_q_lat, g_kv_lat, w_uq, w_ukv, g_mem, w_mem_kv, sink_logit, w_branch_mla, w_branch_win, w_branch_mem, w_out, g_ffn_pre, g_ffn_post, w_ff_up, w_ff_down):
    depth = w_in.shape[0]
    scale = (QK_NOPE + QK_ROPE) ** -0.5 * LOG2E

    w_lat, w_zw, w_g = _prep_in_proj(w_in)
    wq, wkv = _prep_mla(w_uq, w_ukv)
    wmem = w_mem_kv.astype(BF16)
    wa = w_branch_mla.astype(BF16)
    wb = w_branch_win.astype(BF16)
    wc = w_branch_mem.astype(BF16)
    wo = w_out.astype(BF16)
    wup = w_ff_up.astype(BF16)
    wdn = w_ff_down.astype(BF16)
    vec = lambda g: g.reshape(depth, 1, g.shape[1])
    g_pre, g_post, g_q, g_kv = vec(g_attn_pre), vec(g_attn_post), vec(g_q_lat), vec(g_kv_lat)
    g_m, g_fpre, g_fpost = vec(g_mem), vec(g_ffn_pre), vec(g_ffn_post)

    def run(x3, mem3):
        B, S, D = x3.shape
        x = x3.reshape(B * S, D)
        mem = mem3.reshape(B * mem3.shape[1], D)
        h = rmsnorm_bf16(x, g_pre, 0)
        for l in range(depth):
            zw = matmul(h, w_zw, l, BF16, tn=w_zw.shape[2] // 2, name="in_proj_qkv")
            mem_kv = norm_matmul(mem, g_m, wmem, l, BF16)
            q, k, v, b, c = projections_and_local_attention(
                h, zw, mem_kv, sink_logit[l], w_lat, g_q, g_kv, wq, wkv, l, B, S, scale)
            a = mla_attention(q, k, v, B, S)
            merged = gated_merge(h, a, b, c, w_g, wa, wb, wc, l)
            x, h = out_proj(merged, wo, x, g_post, g_fpre, l)
            x, h = ffn(h, wup, wdn, x, g_fpost, g_pre, l, l + 1 if l + 1 < depth else None)
        return x.reshape(B, S, D)

    return (run(x_prompt, mem_prompt), run(x_sample, mem_sample))
```

```python
import functools
import math

import jax
import jax.numpy as jnp
from jax import lax
from jax.experimental import pallas as pl
from jax.experimental.pallas import tpu as pltpu

F32 = jnp.float32
BF16 = jnp.bfloat16

EPS = 1e-6
ROPE_THETA = 10000.0
LOG2E = math.log2(math.e)

MLA_HEADS = 8
QK_NOPE = 128
QK_ROPE = 64
V_DIM = 128
Q_LORA = 512
KV_LORA = 512
WIN_HEADS = 8
WIN_KV_HEADS = 2
WIN_HD = 128
WINDOW = 128
BLOCK = 128
MEM_HEADS = 4
MEM_HD = 256
N_BRANCH = 3

LANES = 128
QK_PAD = 2 * LANES
LAT_W = Q_LORA + KV_LORA + LANES

MERGE_TN = 512
FFN_TF = 1024
VMEM_LIMIT = 56 * 1024 * 1024


def _cparams(sem):
    return pltpu.CompilerParams(dimension_semantics=sem, vmem_limit_bytes=VMEM_LIMIT)


def _rms(x, g):
    r = lax.rsqrt(jnp.mean(x * x, axis=-1, keepdims=True) + EPS)
    return (x * r) * g


def _tile(n, pref):
    t = min(n, pref)
    assert n % t == 0, (n, t)
    return t


def _vec_spec(n, l, nargs):
    if nargs == 1:
        return pl.BlockSpec((None, 1, n), lambda i: (l, 0, 0))
    return pl.BlockSpec((None, 1, n), lambda i, j: (l, 0, 0))


def _norm_kernel(x_ref, g_ref, o_ref):
    o_ref[...] = _rms(x_ref[...], g_ref[...]).astype(o_ref.dtype)


def rmsnorm_bf16(x, g, l):
    T, D = x.shape
    tm = _tile(T, 512)
    return pl.pallas_call(
        _norm_kernel,
        grid=(T // tm,),
        in_specs=[pl.BlockSpec((tm, D), lambda i: (i, 0)), _vec_spec(D, l, 1)],
        out_specs=pl.BlockSpec((tm, D), lambda i: (i, 0)),
        out_shape=jax.ShapeDtypeStruct((T, D), BF16),
        compiler_params=_cparams(("parallel",)),
        name="rmsnorm",
    )(x, g)


def _mm_kernel(a_ref, w_ref, o_ref):
    o_ref[...] = jnp.dot(a_ref[...], w_ref[...], preferred_element_type=F32).astype(o_ref.dtype)


def matmul(a, w, l, out_dtype, tm=1024, tn=None, name="matmul"):
    T, K = a.shape
    N = w.shape[2]
    tm = _tile(T, tm)
    tn = N if tn is None else _tile(N, tn)
    return pl.pallas_call(
        _mm_kernel,
        grid=(T // tm, N // tn),
        in_specs=[pl.BlockSpec((tm, K), lambda i, j: (i, 0)),
                  pl.BlockSpec((None, K, tn), lambda i, j: (l, 0, j))],
        out_specs=pl.BlockSpec((tm, tn), lambda i, j: (i, j)),
        out_shape=jax.ShapeDtypeStruct((T, N), out_dtype),
        compiler_params=_cparams(("parallel", "parallel")),
        name=name,
    )(a, w)


def _norm_mm_kernel(x_ref, g_ref, w_ref, o_ref):
    h = _rms(x_ref[...], g_ref[...]).astype(BF16)
    o_ref[...] = jnp.dot(h, w_ref[...], preferred_element_type=F32).astype(o_ref.dtype)


def norm_matmul(x, g, w, l, out_dtype, tm=256):
    T, K = x.shape
    N = w.shape[2]
    tm = _tile(T, tm)
    return pl.pallas_call(
        _norm_mm_kernel,
        grid=(T // tm,),
        in_specs=[pl.BlockSpec((tm, K), lambda i: (i, 0)),
                  _vec_spec(K, l, 1),
                  pl.BlockSpec((None, K, N), lambda i: (l, 0, 0))],
        out_specs=pl.BlockSpec((tm, N), lambda i: (i, 0)),
        out_shape=jax.ShapeDtypeStruct((T, N), out_dtype),
        compiler_params=_cparams(("parallel",)),
        name="norm_matmul",
    )(x, g, w)


def _rope_tables(S, scale):
    half = QK_ROPE // 2
    inv = 1.0 / (ROPE_THETA ** (jnp.arange(0, QK_ROPE, 2, dtype=F32) / QK_ROPE))
    ang = jnp.arange(S, dtype=F32)[:, None] * inv[None, :]
    cos, sin = jnp.cos(ang) * scale, jnp.sin(ang) * scale
    z32 = jnp.zeros((S, half), F32)
    z64 = jnp.zeros((S, LANES - QK_ROPE), F32)
    cos_t = jnp.concatenate([cos, cos, z64], axis=1)
    sin_lo = jnp.concatenate([-sin, z32, z64], axis=1)
    sin_hi = jnp.concatenate([z32, sin, z64], axis=1)
    return cos_t, sin_lo, sin_hi


def _rope(v, cos_t, sin_lo, sin_hi):
    half = QK_ROPE // 2
    return (v * cos_t + pltpu.roll(v, half, axis=1) * sin_hi
            + pltpu.roll(v, LANES - half, axis=1) * sin_lo)


def _mla_proj_kernel(scale, nsplit, h_ref, wl_ref, gq_ref, gkv_ref, wq_ref, wkv_ref,
                     qcos_ref, qslo_ref, qshi_ref, kcos_ref, kslo_ref, kshi_ref,
                     q_ref, k_ref, v_ref):
    rows = h_ref.shape[0] // nsplit
    nk = MLA_HEADS * QK_NOPE
    lane = lax.broadcasted_iota(jnp.int32, (rows, LANES), 1)
    ones_col = jnp.where(lane == 0, 1.0, 0.0).astype(v_ref.dtype)
    for p in range(nsplit):
        r = slice(p * rows, (p + 1) * rows)
        z = jnp.dot(h_ref[r, :], wl_ref[...], preferred_element_type=F32)
        cq = _rms(z[:, :Q_LORA], gq_ref[...]).astype(BF16)
        ckv = _rms(z[:, Q_LORA:Q_LORA + KV_LORA], gkv_ref[...]).astype(BF16)
        q = jnp.dot(cq, wq_ref[...], preferred_element_type=F32)
        kv = jnp.dot(ckv, wkv_ref[...], preferred_element_type=F32)
        kr = _rope(z[:, Q_LORA + KV_LORA:], kcos_ref[r, :], kslo_ref[r, :], kshi_ref[r, :])
        kr = kr.astype(k_ref.dtype)
        qcos, qslo, qshi = qcos_ref[r, :], qslo_ref[r, :], qshi_ref[r, :]
        for h in range(MLA_HEADS):
            base = h * QK_PAD
            q_ref[r, base:base + LANES] = (q[:, base:base + LANES] * scale).astype(q_ref.dtype)
            q_ref[r, base + LANES:base + QK_PAD] = _rope(
                q[:, base + LANES:base + QK_PAD], qcos, qslo, qshi).astype(q_ref.dtype)
            k_ref[r, base:base + LANES] = kv[:, h * QK_NOPE:(h + 1) * QK_NOPE].astype(k_ref.dtype)
            k_ref[r, base + LANES:base + QK_PAD] = kr
            v_ref[r, base:base + LANES] = kv[:, nk + h * V_DIM:nk + (h + 1) * V_DIM].astype(v_ref.dtype)
            v_ref[r, base + LANES:base + QK_PAD] = ones_col


def _mla_kernel(tq, tk, nchain, unroll, q_ref, k_ref, v_ref, o_ref, sa_ref, sb_ref):
    S = q_ref.shape[0]
    nq, nk = S // (tq * nchain), S // tk
    nsteps = nq * nk
    assert nsteps % unroll == 0 and unroll % 2 == 0

    def rows(qi, c):
        return pl.ds(pl.multiple_of((qi * nchain + c) * tq, tq), tq)

    def scores(t, dst_ref):
        qi = t // nk
        ki = t - qi * nk
        k = k_ref[pl.ds(pl.multiple_of(ki * tk, tk), tk), :]
        for c in range(nchain):
            dst_ref[c] = lax.dot_general(q_ref[rows(qi, c), :], k, (((1,), (1,)), ((), ())),
                                         preferred_element_type=F32)

    def step(t, cur_ref, nxt_ref, carry):
        t_next = jnp.where(t + 1 == nsteps, 0, t + 1)
        scores(t_next, nxt_ref)
        qi = t // nk
        ki = t - qi * nk
        v = v_ref[pl.ds(pl.multiple_of(ki * tk, tk), tk), :]
        out = []
        for c in range(nchain):
            m, acc = carry[c]
            m = jnp.where(ki == 0, -jnp.inf, m)
            s = cur_ref[c]
            m_new = jnp.maximum(m, jnp.max(s, axis=1, keepdims=True))
            alpha = jnp.exp2(m - m_new)
            p = jnp.exp2(s - m_new).astype(BF16)
            acc = alpha * acc + jnp.dot(p, v, preferred_element_type=F32)
            o_ref[rows(qi, c), :] = (acc[:, :V_DIM] / acc[:, V_DIM:V_DIM + 1]).astype(o_ref.dtype)
            out.append((m_new, acc))
        return tuple(out)

    scores(0, sa_ref)

    def body(i, carry):
        for u in range(unroll):
            cur, nxt = (sa_ref, sb_ref) if u % 2 == 0 else (sb_ref, sa_ref)
            carry = step(unroll * i + u, cur, nxt, carry)
        return carry

    init = tuple((jnp.full((tq, 1), -jnp.inf, F32), jnp.zeros((tq, QK_PAD), F32))
                 for _ in range(nchain))
    lax.fori_loop(0, nsteps // unroll, body, init)


def mla_attention(q, k, v, B, S, tq=256, tk=2048, nchain=2):
    T = B * S
    tq, tk = _tile(S, tq), _tile(S, tk)
    assert S % (tq * nchain) == 0
    nsteps = (S // (tq * nchain)) * (S // tk)
    unroll = next(u for u in (8, 4, 2) if nsteps % u == 0)
    return pl.pallas_call(
        functools.partial(_mla_kernel, tq, tk, nchain, unroll),
        grid=(B, MLA_HEADS),
        in_specs=[pl.BlockSpec((S, QK_PAD), lambda b, h: (b, h)),
                  pl.BlockSpec((S, QK_PAD), lambda b, h: (b, h)),
                  pl.BlockSpec((S, QK_PAD), lambda b, h: (b, h))],
        out_specs=pl.BlockSpec((S, V_DIM), lambda b, h: (b, h)),
        out_shape=jax.ShapeDtypeStruct((T, MLA_HEADS * V_DIM), BF16),
        scratch_shapes=[pltpu.VMEM((nchain, tq, tk), F32), pltpu.VMEM((nchain, tq, tk), F32)],
        compiler_params=_cparams(("parallel", "parallel")),
        name="mla_attention",
    )(q, k, v)


def _window_bias_table():
    G = WIN_HEADS // WIN_KV_HEADS
    qoff = jnp.arange(BLOCK)[:, None]
    col = jnp.arange(3 * BLOCK)[None, :]
    dist = jnp.abs(qoff + BLOCK - col)
    slopes = jnp.asarray([2.0 ** (-8.0 * (h + 1) / WIN_HEADS) for h in range(WIN_HEADS)], F32)
    bias = -slopes[:, None, None] * dist.astype(F32)[None] * LOG2E
    bias = jnp.where((dist <= WINDOW)[None], bias, -jnp.inf)
    return bias.reshape(WIN_KV_HEADS, G * BLOCK, 3 * BLOCK)


def _win_kernel(nb, tq, bias_ref, sink_ref, q_ref, kp_ref, kc_ref, kn_ref,
                vp_ref, vc_ref, vn_ref, o_ref):
    i = pl.program_id(1)
    G = WIN_HEADS // WIN_KV_HEADS
    r = tq // BLOCK
    kb = jnp.concatenate([kp_ref[...], kc_ref[...], kn_ref[...]], axis=0)
    vb = jnp.concatenate([vp_ref[...], vc_ref[...], vn_ref[...]], axis=0)
    col = lax.broadcasted_iota(jnp.int32, (1, 3 * BLOCK), 1)
    gidx = lax.broadcasted_iota(jnp.int32, (G * BLOCK, 1), 0) // BLOCK
    sinks = []
    for kvh in range(WIN_KV_HEADS):
        sk = jnp.zeros((G * BLOCK, 1), F32)
        for g in range(G):
            sk = jnp.where(gidx == g, sink_ref[kvh * G + g] * LOG2E, sk)
        sinks.append(sk)
    lane = lax.broadcasted_iota(jnp.int32, (3 * BLOCK, WIN_HD), 1)
    ones_col = jnp.where(lane == 0, 1.0, 0.0).astype(vb.dtype)
    for j in range(r):
        edge = None
        if j == 0:
            edge = jnp.where((i == 0) & (col < BLOCK), -jnp.inf, 0.0)
        if j == r - 1:
            hi = jnp.where((i * r + j == nb - 1) & (col >= 2 * BLOCK), -jnp.inf, 0.0)
            edge = hi if edge is None else edge + hi
        for kvh in range(WIN_KV_HEADS):
            qs = jnp.concatenate(
                [q_ref[j * BLOCK:(j + 1) * BLOCK, (kvh * G + g) * WIN_HD:(kvh * G + g + 1) * WIN_HD]
                 for g in range(G)], axis=0)
            kk = kb[j * BLOCK:(j + 3) * BLOCK, kvh * WIN_HD:(kvh + 1) * WIN_HD]
            vv = jnp.concatenate(
                [vb[j * BLOCK:(j + 3) * BLOCK, kvh * WIN_HD:(kvh + 1) * WIN_HD], ones_col], axis=1)
            s = lax.dot_general(qs, kk, (((1,), (1,)), ((), ())), preferred_element_type=F32)
            s = s * (WIN_HD ** -0.5 * LOG2E) + bias_ref[kvh]
            if edge is not None:
                s = s + edge
            sink = sinks[kvh]
            m = jnp.maximum(jnp.max(s, axis=1, keepdims=True), sink)
            ev = jnp.dot(jnp.exp2(s - m).astype(BF16), vv, preferred_element_type=F32)
            o = ev[:, :WIN_HD] / (ev[:, WIN_HD:WIN_HD + 1] + jnp.exp2(sink - m))
            for g in range(G):
                c0 = (kvh * G + g) * WIN_HD
                o_ref[j * BLOCK:(j + 1) * BLOCK, c0:c0 + WIN_HD] = (
                    o[g * BLOCK:(g + 1) * BLOCK].astype(o_ref.dtype))


def _mem_kernel(q_ref, kv_ref, o_ref):
    W = MEM_HEADS * MEM_HD
    for h in range(MEM_HEADS):
        q = q_ref[:, h * MEM_HD:(h + 1) * MEM_HD]
        k = kv_ref[:, h * MEM_HD:(h + 1) * MEM_HD]
        v = kv_ref[:, W + h * MEM_HD:W + (h + 1) * MEM_HD]
        s = lax.dot_general(q, k, (((1,), (1,)), ((), ())), preferred_element_type=F32)
        s = s * (MEM_HD ** -0.5 * LOG2E)
        m = jnp.max(s, axis=1, keepdims=True)
        e = jnp.exp2(s - m)
        p = e / jnp.sum(e, axis=1, keepdims=True)
        o_ref[:, h * MEM_HD:(h + 1) * MEM_HD] = jnp.dot(
            p.astype(BF16), v, preferred_element_type=F32).astype(o_ref.dtype)


N_PROJ_IN, N_WIN_IN, N_MEM_IN = 12, 9, 2


def _side_kernel(scale, nsplit, nb, tq, *refs):
    proj_in = refs[:N_PROJ_IN]
    win_in = refs[N_PROJ_IN:N_PROJ_IN + N_WIN_IN]
    mem_in = refs[N_PROJ_IN + N_WIN_IN:N_PROJ_IN + N_WIN_IN + N_MEM_IN]
    q_ref, k_ref, v_ref, win_o_ref, mem_o_ref = refs[N_PROJ_IN + N_WIN_IN + N_MEM_IN:]
    _mla_proj_kernel(scale, nsplit, *proj_in, q_ref, k_ref, v_ref)
    _win_kernel(nb, tq, *win_in, win_o_ref)
    _mem_kernel(*mem_in, mem_o_ref)


def projections_and_local_attention(h, zw, mem_kv, sink, w_lat, g_q, g_kv, w_qm, w_kv,
                                    l, B, S, scale, tq=512, nsplit=2):
    T, D = h.shape
    tq = _tile(S, tq)
    nq = S // tq
    r = tq // BLOCK
    nb = S // BLOCK
    KW = WIN_KV_HEADS * WIN_HD
    WW = WIN_HEADS * WIN_HD
    MW = MEM_HEADS * MEM_HD
    kcol = (WW + MW) // KW
    vcol = kcol + 1
    M = mem_kv.shape[0] // B
    N = MLA_HEADS * QK_PAD
    bias = _window_bias_table()
    q_tabs = _rope_tables(S, scale)
    k_tabs = _rope_tables(S, 1.0)

    tile = lambda w, c=0: pl.BlockSpec((tq, w), lambda b, i: (b * nq + i, c))
    tab = pl.BlockSpec((tq, LANES), lambda b, i: (i, 0))
    full = lambda a: pl.BlockSpec((None,) + a.shape[1:], lambda b, i: (l, 0, 0))
    prev = lambda c: pl.BlockSpec((BLOCK, KW), lambda b, i: (b * nb + jnp.maximum(i * r - 1, 0), c))
    nxt = lambda c: pl.BlockSpec((BLOCK, KW), lambda b, i: (b * nb + jnp.minimum((i + 1) * r, nb - 1), c))

    proj_specs = [tile(D), full(w_lat), _vec_spec(Q_LORA, l, 2), _vec_spec(KV_LORA, l, 2),
                  full(w_qm), full(w_kv), tab, tab, tab, tab, tab, tab]
    win_specs = [pl.BlockSpec(bias.shape, lambda b, i: (0, 0, 0)),
                 pl.BlockSpec(memory_space=pltpu.SMEM),
                 tile(WW), prev(kcol), tile(KW, kcol), nxt(kcol), prev(vcol), tile(KW, vcol), nxt(vcol)]
    mem_specs = [tile(MW, 1), pl.BlockSpec((M, 2 * MW), lambda b, i: (b, 0))]
    assert (len(proj_specs), len(win_specs), len(mem_specs)) == (N_PROJ_IN, N_WIN_IN, N_MEM_IN)
    return pl.pallas_call(
        functools.partial(_side_kernel, scale, nsplit, nb, tq),
        grid=(B, nq),
        in_specs=proj_specs + win_specs + mem_specs,
        out_specs=[tile(N), tile(N), tile(N), tile(WW), tile(MW)],
        out_shape=[jax.ShapeDtypeStruct((T, N), BF16)] * 3 + [
            jax.ShapeDtypeStruct((T, WW), BF16), jax.ShapeDtypeStruct((T, MW), BF16)],
        compiler_params=_cparams(("parallel", "parallel")),
        name="proj_local_attn",
    )(h, w_lat, g_q, g_kv, w_qm, w_kv, *q_tabs, *k_tabs,
      bias, sink.astype(F32), zw, zw, zw, zw, zw, zw, zw,
      zw, mem_kv)


def _merge_kernel(h_ref, a_ref, b_ref, c_ref, wga_ref, wgb_ref, wgc_ref,
                  wa_ref, wb_ref, wc_ref, o_ref):
    h = h_ref[...]

    def branch(x_ref, wg_ref, w_ref):
        gate = jax.nn.sigmoid(jnp.dot(h, wg_ref[...], preferred_element_type=F32))
        return gate * jnp.dot(x_ref[...], w_ref[...], preferred_element_type=F32)

    o_ref[...] = (branch(a_ref, wga_ref, wa_ref) + branch(b_ref, wgb_ref, wb_ref)
                  + branch(c_ref, wgc_ref, wc_ref)).astype(o_ref.dtype)


def gated_merge(h, a, b, c, w_g, w_a, w_b, w_c, l, tm=1024, tn=MERGE_TN):
    T, D = h.shape
    tm, tn = _tile(T, tm), _tile(D, tn)
    nj = D // tn
    act = lambda w: pl.BlockSpec((tm, w), lambda i, j: (i, 0))
    wg = lambda br: pl.BlockSpec((None, D, tn), lambda i, j: (l, 0, br * nj + j))
    wx = lambda k: pl.BlockSpec((None, k, tn), lambda i, j: (l, 0, j))
    return pl.pallas_call(
        _merge_kernel,
        grid=(T // tm, nj),
        in_specs=[act(D), act(a.shape[1]), act(b.shape[1]), act(c.shape[1]),
                  wg(0), wg(1), wg(2),
                  wx(a.shape[1]), wx(b.shape[1]), wx(c.shape[1])],
        out_specs=pl.BlockSpec((tm, tn), lambda i, j: (i, j)),
        out_shape=jax.ShapeDtypeStruct((T, D), BF16),
        compiler_params=_cparams(("parallel", "parallel")),
        name="gated_merge",
    )(h, a, b, c, w_g, w_g, w_g, w_a, w_b, w_c)


def _outproj_kernel(nsplit, m_ref, w_ref, x_ref, gpost_ref, gnext_ref, xo_ref, ho_ref):
    rows = m_ref.shape[0] // nsplit
    for p in range(nsplit):
        r = slice(p * rows, (p + 1) * rows)
        y = jnp.dot(m_ref[r, :], w_ref[...], preferred_element_type=F32)
        x_new = x_ref[r, :] + _rms(y, gpost_ref[...])
        xo_ref[r, :] = x_new
        ho_ref[r, :] = _rms(x_new, gnext_ref[...]).astype(ho_ref.dtype)


def out_proj(merged, w_out, x, g_post, g_next, l, tm=512, nsplit=4):
    T, D = x.shape
    tm = _tile(T, tm)
    row = lambda: pl.BlockSpec((tm, D), lambda i: (i, 0))
    return pl.pallas_call(
        functools.partial(_outproj_kernel, nsplit),
        grid=(T // tm,),
        in_specs=[row(), pl.BlockSpec((None, D, D), lambda i: (l, 0, 0)), row(),
                  _vec_spec(D, l, 1), _vec_spec(D, l, 1)],
        out_specs=[row(), row()],
        out_shape=[jax.ShapeDtypeStruct((T, D), F32), jax.ShapeDtypeStruct((T, D), BF16)],
        compiler_params=_cparams(("parallel",)),
        name="out_proj",
    )(merged, w_out, x, g_post, g_next)


def _ffn_kernel(emit_h, nsplit, nf, h_ref, wu_ref, wd_ref, x_ref, gpost_ref, gnext_ref, xo_ref, *rest):
    ho_ref, acc_ref = rest if emit_h else (None, rest[0])
    k = pl.program_id(1)
    last = pl.num_programs(1) - 1

    def partial_sum(r):
        u = jnp.dot(h_ref[r, :], wu_ref[...], preferred_element_type=F32)
        u = jnp.square(jnp.maximum(u, 0.0)).astype(BF16)
        return jnp.dot(u, wd_ref[...], preferred_element_type=F32)

    if nf > 1:
        @pl.when(k == 0)
        def _():
            acc_ref[...] = partial_sum(slice(None))

        @pl.when((k > 0) & (k < last))
        def _():
            acc_ref[...] += partial_sum(slice(None))

    @pl.when(k == last)
    def _():
        rows = h_ref.shape[0] // nsplit
        for p in range(nsplit):
            r = slice(p * rows, (p + 1) * rows)
            y = partial_sum(r)
            if nf > 1:
                y = acc_ref[r, :] + y
            x_new = x_ref[r, :] + _rms(y, gpost_ref[...])
            xo_ref[r, :] = x_new
            if emit_h:
                ho_ref[r, :] = _rms(x_new, gnext_ref[...]).astype(ho_ref.dtype)


def ffn(h, w_up, w_down, x, g_post, g_next, l, l_next, tm=512, tf=FFN_TF, nsplit=2):
    T, D = x.shape
    Fd = w_up.shape[2]
    tm, tf = _tile(T, tm), _tile(Fd, tf)
    nf = Fd // tf
    emit_h = l_next is not None
    row = lambda: pl.BlockSpec((tm, D), lambda i, k: (i, 0))
    out_specs = [row()] + ([row()] if emit_h else [])
    out_shape = [jax.ShapeDtypeStruct((T, D), F32)] + (
        [jax.ShapeDtypeStruct((T, D), BF16)] if emit_h else [])
    res = pl.pallas_call(
        functools.partial(_ffn_kernel, emit_h, nsplit, nf),
        grid=(T // tm, nf),
        in_specs=[row(),
                  pl.BlockSpec((None, D, tf), lambda i, k: (l, 0, k)),
                  pl.BlockSpec((None, tf, D), lambda i, k: (l, k, 0)),
                  row(), _vec_spec(D, l, 2), _vec_spec(D, l_next if emit_h else l, 2)],
        out_specs=out_specs,
        out_shape=out_shape,
        scratch_shapes=[pltpu.VMEM((tm, D), F32)],
        compiler_params=_cparams(("parallel", "arbitrary")),
        name="ffn",
    )(h, w_up, w_down, x, g_post, g_next)
    return (res[0], res[1]) if emit_h else (res[0], None)


def _prep_in_proj(w_in):
    L, D, _ = w_in.shape
    o_lat = Q_LORA + KV_LORA + QK_ROPE
    o_qw = o_lat + WIN_HEADS * WIN_HD
    o_kw = o_qw + WIN_KV_HEADS * WIN_HD
    o_vw = o_kw + WIN_KV_HEADS * WIN_HD
    o_qm = o_vw + MEM_HEADS * MEM_HD
    w = w_in
    w_lat = jnp.concatenate([w[:, :, :o_lat], jnp.zeros((L, D, LAT_W - o_lat), w.dtype)], axis=2)
    w_zw = jnp.concatenate([w[:, :, o_lat:o_qw], w[:, :, o_vw:o_qm],
                            w[:, :, o_qw:o_kw], w[:, :, o_kw:o_vw]], axis=2)
    w_g = w[:, :, o_qm:]
    return w_lat.astype(BF16), w_zw.astype(BF16), w_g.astype(BF16)


def _prep_mla(w_uq, w_ukv):
    L = w_uq.shape[0]
    wq = w_uq.astype(BF16).reshape(L, Q_LORA, MLA_HEADS, QK_NOPE + QK_ROPE)
    wq = jnp.pad(wq, ((0, 0), (0, 0), (0, 0), (0, QK_PAD - QK_NOPE - QK_ROPE)))
    wq = wq.reshape(L, Q_LORA, MLA_HEADS * QK_PAD)
    wkv = w_ukv.astype(BF16).reshape(L, KV_LORA, MLA_HEADS, QK_NOPE + V_DIM)
    wkv = jnp.concatenate([wkv[..., :QK_NOPE].reshape(L, KV_LORA, -1),
                           wkv[..., QK_NOPE:].reshape(L, KV_LORA, -1)], axis=2)
    return wq, wkv


def kernel(x_prompt, x_sample, mem_prompt, mem_sample, g_attn_pre, g_attn_post, w_in, g_q_lat, g_kv_lat, w_uq, w_ukv, g_mem, w_mem_kv, sink_logit, w_branch_mla, w_branch_win, w_branch_mem, w_out, g_ffn_pre, g_ffn_post, w_ff_up, w_ff_down):
    depth = w_in.shape[0]
    scale = (QK_NOPE + QK_ROPE) ** -0.5 * LOG2E

    w_lat, w_zw, w_g = _prep_in_proj(w_in)
    wq, wkv = _prep_mla(w_uq, w_ukv)
    wmem = w_mem_kv.astype(BF16)
    wa = w_branch_mla.astype(BF16)
    wb = w_branch_win.astype(BF16)
    wc = w_branch_mem.astype(BF16)
    wo = w_out.astype(BF16)
    wup = w_ff_up.astype(BF16)
    wdn = w_ff_down.astype(BF16)
    vec = lambda g: g.reshape(depth, 1, g.shape[1])
    g_pre, g_post, g_q, g_kv = vec(g_attn_pre), vec(g_attn_post), vec(g_q_lat), vec(g_kv_lat)
    g_m, g_fpre, g_fpost = vec(g_mem), vec(g_ffn_pre), vec(g_ffn_post)

    def run(x3, mem3):
        B, S, D = x3.shape
        x = x3.reshape(B * S, D)
        mem = mem3.reshape(B * mem3.shape[1], D)
        h = rmsnorm_bf16(x, g_pre, 0)
        for l in range(depth):
            zw = matmul(h, w_zw, l, BF16, tn=w_zw.shape[2] // 2, name="in_proj_qkv")
            mem_kv = norm_matmul(mem, g_m, wmem, l, BF16)
            q, k, v, b, c = projections_and_local_attention(
                h, zw, mem_kv, sink_logit[l], w_lat, g_q, g_kv, wq, wkv, l, B, S, scale)
            a = mla_attention(q, k, v, B, S)
            merged = gated_merge(h, a, b, c, w_g, wa, wb, wc, l)
            x, h = out_proj(merged, wo, x, g_post, g_fpre, l)
            x, h = ffn(h, wup, wdn, x, g_fpost, g_pre, l, l + 1 if l + 1 < depth else None)
        return x.reshape(B, S, D)

    return (run(x_prompt, mem_prompt), run(x_sample, mem_sample))
```

```python
import functools
import math

import jax
import jax.numpy as jnp
from jax import lax
from jax.experimental import pallas as pl
from jax.experimental.pallas import tpu as pltpu

F32 = jnp.float32
BF16 = jnp.bfloat16

EPS = 1e-6
ROPE_THETA = 10000.0
LOG2E = math.log2(math.e)

MLA_HEADS = 8
QK_NOPE = 128
QK_ROPE = 64
V_DIM = 128
Q_LORA = 512
KV_LORA = 512
WIN_HEADS = 8
WIN_KV_HEADS = 2
WIN_HD = 128
WINDOW = 128
BLOCK = 128
MEM_HEADS = 4
MEM_HD = 256
N_BRANCH = 3

LANES = 128
QK_PAD = 2 * LANES
LAT_W = 1280
QKV_W = WIN_HEADS * WIN_HD + 2 * WIN_KV_HEADS * WIN_HD + MEM_HEADS * MEM_HD
GATE_OFF = 4096

MERGE_TN = 512
FFN_TF = 1024
MLA_TK_SINGLE = 4096
MLA_TK_CHUNK = 2048
VMEM_LIMIT = 56 * 1024 * 1024


def _cparams(sem):
    return pltpu.CompilerParams(dimension_semantics=sem, vmem_limit_bytes=VMEM_LIMIT)


def _rms(x, g):
    r = lax.rsqrt(jnp.mean(x * x, axis=-1, keepdims=True) + EPS)
    return (x * r) * g


def _tile(n, pref):
    t = min(n, pref)
    assert n % t == 0, (n, t)
    return t


def _vec_spec(n, l, nargs):
    if nargs == 1:
        return pl.BlockSpec((None, 1, n), lambda i: (l, 0, 0))
    return pl.BlockSpec((None, 1, n), lambda i, j: (l, 0, 0))


def _norm_kernel(x_ref, g_ref, o_ref):
    o_ref[...] = _rms(x_ref[...], g_ref[...]).astype(o_ref.dtype)


def rmsnorm_bf16(x, g, l):
    T, D = x.shape
    tm = _tile(T, 512)
    return pl.pallas_call(
        _norm_kernel,
        grid=(T // tm,),
        in_specs=[pl.BlockSpec((tm, D), lambda i: (i, 0)), _vec_spec(D, l, 1)],
        out_specs=pl.BlockSpec((tm, D), lambda i: (i, 0)),
        out_shape=jax.ShapeDtypeStruct((T, D), BF16),
        compiler_params=_cparams(("parallel",)),
        name="rmsnorm",
    )(x, g)


def _mm_kernel(a_ref, w_ref, o_ref):
    o_ref[...] = jnp.dot(a_ref[...], w_ref[...], preferred_element_type=F32).astype(o_ref.dtype)


def matmul(a, w, l, out_dtype, col0, N, tm=1024, tn=None, name="matmul"):
    T, K = a.shape
    tm = _tile(T, tm)
    tn = N if tn is None else _tile(N, tn)
    assert col0 % tn == 0
    return pl.pallas_call(
        _mm_kernel,
        grid=(T // tm, N // tn),
        in_specs=[pl.BlockSpec((tm, K), lambda i, j: (i, 0)),
                  pl.BlockSpec((None, K, tn), lambda i, j: (l, 0, col0 // tn + j))],
        out_specs=pl.BlockSpec((tm, tn), lambda i, j: (i, j)),
        out_shape=jax.ShapeDtypeStruct((T, N), out_dtype),
        compiler_params=_cparams(("parallel", "parallel")),
        name=name,
    )(a, w)


def _norm_mm_kernel(x_ref, g_ref, w_ref, o_ref):
    h = _rms(x_ref[...], g_ref[...]).astype(BF16)
    o_ref[...] = jnp.dot(h, w_ref[...], preferred_element_type=F32).astype(o_ref.dtype)


def norm_matmul(x, g, w, l, out_dtype, tm=256):
    T, K = x.shape
    N = w.shape[2]
    tm = _tile(T, tm)
    return pl.pallas_call(
        _norm_mm_kernel,
        grid=(T // tm,),
        in_specs=[pl.BlockSpec((tm, K), lambda i: (i, 0)),
                  _vec_spec(K, l, 1),
                  pl.BlockSpec((None, K, N), lambda i: (l, 0, 0))],
        out_specs=pl.BlockSpec((tm, N), lambda i: (i, 0)),
        out_shape=jax.ShapeDtypeStruct((T, N), out_dtype),
        compiler_params=_cparams(("parallel",)),
        name="norm_matmul",
    )(x, g, w)


def _rope_tables(S, scale):
    half = QK_ROPE // 2
    inv = 1.0 / (ROPE_THETA ** (jnp.arange(0, QK_ROPE, 2, dtype=F32) / QK_ROPE))
    ang = jnp.arange(S, dtype=F32)[:, None] * inv[None, :]
    cos, sin = jnp.cos(ang) * scale, jnp.sin(ang) * scale
    z32 = jnp.zeros((S, half), F32)
    z64 = jnp.zeros((S, LANES - QK_ROPE), F32)
    cos_t = jnp.concatenate([cos, cos, z64], axis=1)
    sin_lo = jnp.concatenate([-sin, z32, z64], axis=1)
    sin_hi = jnp.concatenate([z32, sin, z64], axis=1)
    return cos_t, sin_lo, sin_hi


def _rope(v, cos_t, sin_lo, sin_hi):
    half = QK_ROPE // 2
    return (v * cos_t + pltpu.roll(v, half, axis=1) * sin_hi
            + pltpu.roll(v, LANES - half, axis=1) * sin_lo)


def _mla_proj_kernel(scale, nsplit, h_ref, wl_ref, gq_ref, gkv_ref, wq_ref, wkv_ref,
                     qcos_ref, qslo_ref, qshi_ref, kcos_ref, kslo_ref, kshi_ref,
                     q_ref, k_ref, v_ref):
    rows = h_ref.shape[0] // nsplit
    nk = MLA_HEADS * QK_NOPE
    lane = lax.broadcasted_iota(jnp.int32, (rows, LANES), 1)
    ones_col = jnp.where(lane == 0, 1.0, 0.0).astype(v_ref.dtype)
    for p in range(nsplit):
        r = slice(p * rows, (p + 1) * rows)
        z = jnp.dot(h_ref[r, :], wl_ref[...], preferred_element_type=F32)
        cq = _rms(z[:, :Q_LORA], gq_ref[...]).astype(BF16)
        ckv = _rms(z[:, Q_LORA:Q_LORA + KV_LORA], gkv_ref[...]).astype(BF16)
        q = jnp.dot(cq, wq_ref[...], preferred_element_type=F32)
        kv = jnp.dot(ckv, wkv_ref[...], preferred_element_type=F32)
        kr = _rope(z[:, Q_LORA + KV_LORA:Q_LORA + KV_LORA + LANES],
                   kcos_ref[r, :], kslo_ref[r, :], kshi_ref[r, :])
        kr = kr.astype(k_ref.dtype)
        qcos, qslo, qshi = qcos_ref[r, :], qslo_ref[r, :], qshi_ref[r, :]
        for h in range(MLA_HEADS):
            base = h * QK_PAD
            q_ref[r, base:base + LANES] = (q[:, base:base + LANES] * scale).astype(q_ref.dtype)
            q_ref[r, base + LANES:base + QK_PAD] = _rope(
                q[:, base + LANES:base + QK_PAD], qcos, qslo, qshi).astype(q_ref.dtype)
            k_ref[r, base:base + LANES] = kv[:, h * QK_NOPE:(h + 1) * QK_NOPE].astype(k_ref.dtype)
            k_ref[r, base + LANES:base + QK_PAD] = kr
            v_ref[r, base:base + LANES] = kv[:, nk + h * V_DIM:nk + (h + 1) * V_DIM].astype(v_ref.dtype)
            v_ref[r, base + LANES:base + QK_PAD] = ones_col


def _mla_kernel(tq, tk, nchain, unroll, q_ref, k_ref, v_ref, o_ref, sa_ref, sb_ref):
    S = q_ref.shape[0]
    nq, nk = S // (tq * nchain), S // tk
    nsteps = nq * nk
    assert nsteps % unroll == 0 and unroll % 2 == 0

    def rows(qi, c):
        return pl.ds(pl.multiple_of((qi * nchain + c) * tq, tq), tq)

    def scores(t, dst_ref):
        qi = t // nk
        ki = t - qi * nk
        k = k_ref[pl.ds(pl.multiple_of(ki * tk, tk), tk), :]
        for c in range(nchain):
            dst_ref[c] = lax.dot_general(q_ref[rows(qi, c), :], k, (((1,), (1,)), ((), ())),
                                         preferred_element_type=F32)

    def step(t, cur_ref, nxt_ref, carry):
        t_next = jnp.where(t + 1 == nsteps, 0, t + 1)
        scores(t_next, nxt_ref)
        qi = t // nk
        ki = t - qi * nk
        v = v_ref[pl.ds(pl.multiple_of(ki * tk, tk), tk), :]
        out = []
        for c in range(nchain):
            m, acc = carry[c]
            m = jnp.where(ki == 0, -jnp.inf, m)
            s = cur_ref[c]
            m_new = jnp.maximum(m, jnp.max(s, axis=1, keepdims=True))
            alpha = jnp.exp2(m - m_new)
            p = jnp.exp2(s - m_new).astype(BF16)
            acc = alpha * acc + jnp.dot(p, v, preferred_element_type=F32)
            o_ref[rows(qi, c), :] = (acc[:, :V_DIM] / acc[:, V_DIM:V_DIM + 1]).astype(o_ref.dtype)
            out.append((m_new, acc))
        return tuple(out)

    scores(0, sa_ref)

    def body(i, carry):
        for u in range(unroll):
            cur, nxt = (sa_ref, sb_ref) if u % 2 == 0 else (sb_ref, sa_ref)
            carry = step(unroll * i + u, cur, nxt, carry)
        return carry

    init = tuple((jnp.full((tq, 1), -jnp.inf, F32), jnp.zeros((tq, QK_PAD), F32))
                 for _ in range(nchain))
    lax.fori_loop(0, nsteps // unroll, body, init)


def mla_attention(q, k, v, B, S, tq=256, nchain=2):
    T = B * S
    tq = _tile(S, tq)
    tk = S if S <= MLA_TK_SINGLE else _tile(S, MLA_TK_CHUNK)
    assert S % (tq * nchain) == 0
    nsteps = (S // (tq * nchain)) * (S // tk)
    unroll = next(u for u in (8, 4, 2) if nsteps % u == 0)
    return pl.pallas_call(
        functools.partial(_mla_kernel, tq, tk, nchain, unroll),
        grid=(B, MLA_HEADS),
        in_specs=[pl.BlockSpec((S, QK_PAD), lambda b, h: (b, h)),
                  pl.BlockSpec((S, QK_PAD), lambda b, h: (b, h)),
                  pl.BlockSpec((S, QK_PAD), lambda b, h: (b, h))],
        out_specs=pl.BlockSpec((S, V_DIM), lambda b, h: (b, h)),
        out_shape=jax.ShapeDtypeStruct((T, MLA_HEADS * V_DIM), BF16),
        scratch_shapes=[pltpu.VMEM((nchain, tq, tk), F32), pltpu.VMEM((nchain, tq, tk), F32)],
        compiler_params=_cparams(("parallel", "parallel")),
        name="mla_attention",
    )(q, k, v)


def _window_bias_table():
    G = WIN_HEADS // WIN_KV_HEADS
    qoff = jnp.arange(BLOCK)[:, None]
    col = jnp.arange(3 * BLOCK)[None, :]
    dist = jnp.abs(qoff + BLOCK - col)
    slopes = jnp.asarray([2.0 ** (-8.0 * (h + 1) / WIN_HEADS) for h in range(WIN_HEADS)], F32)
    bias = -slopes[:, None, None] * dist.astype(F32)[None] * LOG2E
    bias = jnp.where((dist <= WINDOW)[None], bias, -jnp.inf)
    return bias.reshape(WIN_KV_HEADS, G * BLOCK, 3 * BLOCK)


def _win_kernel(nb, tq, bias_ref, sink_ref, q_ref, kp_ref, kc_ref, kn_ref,
                vp_ref, vc_ref, vn_ref, o_ref):
    i = pl.program_id(1)
    G = WIN_HEADS // WIN_KV_HEADS
    r = tq // BLOCK
    kb = jnp.concatenate([kp_ref[...], kc_ref[...], kn_ref[...]], axis=0)
    vb = jnp.concatenate([vp_ref[...], vc_ref[...], vn_ref[...]], axis=0)
    col = lax.broadcasted_iota(jnp.int32, (1, 3 * BLOCK), 1)
    gidx = lax.broadcasted_iota(jnp.int32, (G * BLOCK, 1), 0) // BLOCK
    sinks = []
    for kvh in range(WIN_KV_HEADS):
        sk = jnp.zeros((G * BLOCK, 1), F32)
        for g in range(G):
            sk = jnp.where(gidx == g, sink_ref[kvh * G + g] * LOG2E, sk)
        sinks.append(sk)
    lane = lax.broadcasted_iota(jnp.int32, (3 * BLOCK, WIN_HD), 1)
    ones_col = jnp.where(lane == 0, 1.0, 0.0).astype(vb.dtype)
    for j in range(r):
        edge = None
        if j == 0:
            edge = jnp.where((i == 0) & (col < BLOCK), -jnp.inf, 0.0)
        if j == r - 1:
            hi = jnp.where((i * r + j == nb - 1) & (col >= 2 * BLOCK), -jnp.inf, 0.0)
            edge = hi if edge is None else edge + hi
        for kvh in range(WIN_KV_HEADS):
            qs = jnp.concatenate(
                [q_ref[j * BLOCK:(j + 1) * BLOCK, (kvh * G + g) * WIN_HD:(kvh * G + g + 1) * WIN_HD]
                 for g in range(G)], axis=0)
            kk = kb[j * BLOCK:(j + 3) * BLOCK, kvh * WIN_HD:(kvh + 1) * WIN_HD]
            vv = jnp.concatenate(
                [vb[j * BLOCK:(j + 3) * BLOCK, kvh * WIN_HD:(kvh + 1) * WIN_HD], ones_col], axis=1)
            s = lax.dot_general(qs, kk, (((1,), (1,)), ((), ())), preferred_element_type=F32)
            s = s * (WIN_HD ** -0.5 * LOG2E) + bias_ref[kvh]
            if edge is not None:
                s = s + edge
            sink = sinks[kvh]
            m = jnp.maximum(jnp.max(s, axis=1, keepdims=True), sink)
            ev = jnp.dot(jnp.exp2(s - m).astype(BF16), vv, preferred_element_type=F32)
            o = ev[:, :WIN_HD] / (ev[:, WIN_HD:WIN_HD + 1] + jnp.exp2(sink - m))
            for g in range(G):
                c0 = (kvh * G + g) * WIN_HD
                o_ref[j * BLOCK:(j + 1) * BLOCK, c0:c0 + WIN_HD] = (
                    o[g * BLOCK:(g + 1) * BLOCK].astype(o_ref.dtype))


def _mem_kernel(*refs):
    q_refs, kv_ref, o_ref = refs[:MEM_HEADS], refs[MEM_HEADS], refs[MEM_HEADS + 1]
    W = MEM_HEADS * MEM_HD
    for h in range(MEM_HEADS):
        q = q_refs[h][...]
        k = kv_ref[:, h * MEM_HD:(h + 1) * MEM_HD]
        v = kv_ref[:, W + h * MEM_HD:W + (h + 1) * MEM_HD]
        s = lax.dot_general(q, k, (((1,), (1,)), ((), ())), preferred_element_type=F32)
        s = s * (MEM_HD ** -0.5 * LOG2E)
        m = jnp.max(s, axis=1, keepdims=True)
        e = jnp.exp2(s - m)
        p = e / jnp.sum(e, axis=1, keepdims=True)
        o_ref[:, h * MEM_HD:(h + 1) * MEM_HD] = jnp.dot(
            p.astype(BF16), v, preferred_element_type=F32).astype(o_ref.dtype)


N_PROJ_IN, N_WIN_IN, N_MEM_IN = 12, 9, MEM_HEADS + 1


def _side_kernel(scale, nsplit, nb, tq, *refs):
    proj_in = refs[:N_PROJ_IN]
    win_in = refs[N_PROJ_IN:N_PROJ_IN + N_WIN_IN]
    mem_in = refs[N_PROJ_IN + N_WIN_IN:N_PROJ_IN + N_WIN_IN + N_MEM_IN]
    q_ref, k_ref, v_ref, win_o_ref, mem_o_ref = refs[N_PROJ_IN + N_WIN_IN + N_MEM_IN:]
    _mla_proj_kernel(scale, nsplit, *proj_in, q_ref, k_ref, v_ref)
    _win_kernel(nb, tq, *win_in, win_o_ref)
    _mem_kernel(*mem_in, mem_o_ref)


def projections_and_local_attention(h, zw, mem_kv, sink, w_all, g_q, g_kv, w_qm, w_kv,
                                    l, B, S, scale, tq=512, nsplit=2):
    T, D = h.shape
    tq = _tile(S, tq)
    nq = S // tq
    r = tq // BLOCK
    nb = S // BLOCK
    KW = WIN_KV_HEADS * WIN_HD
    WW = WIN_HEADS * WIN_HD
    MW = MEM_HEADS * MEM_HD
    kcol = WW // KW
    vcol = kcol + 1
    mcol = (WW + 2 * KW) // MEM_HD
    M = mem_kv.shape[0] // B
    N = MLA_HEADS * QK_PAD
    bias = _window_bias_table()
    q_tabs = _rope_tables(S, scale)
    k_tabs = _rope_tables(S, 1.0)

    tile = lambda w, c=0: pl.BlockSpec((tq, w), lambda b, i: (b * nq + i, c))
    tab = pl.BlockSpec((tq, LANES), lambda b, i: (i, 0))
    full = lambda a: pl.BlockSpec((None,) + a.shape[1:], lambda b, i: (l, 0, 0))
    lat = pl.BlockSpec((None, D, LAT_W), lambda b, i: (l, 0, 0))
    prev = lambda c: pl.BlockSpec((BLOCK, KW), lambda b, i: (b * nb + jnp.maximum(i * r - 1, 0), c))
    nxt = lambda c: pl.BlockSpec((BLOCK, KW), lambda b, i: (b * nb + jnp.minimum((i + 1) * r, nb - 1), c))

    proj_specs = [tile(D), lat, _vec_spec(Q_LORA, l, 2), _vec_spec(KV_LORA, l, 2),
                  full(w_qm), full(w_kv), tab, tab, tab, tab, tab, tab]
    win_specs = [pl.BlockSpec(bias.shape, lambda b, i: (0, 0, 0)),
                 pl.BlockSpec(memory_space=pltpu.SMEM),
                 tile(WW), prev(kcol), tile(KW, kcol), nxt(kcol), prev(vcol), tile(KW, vcol), nxt(vcol)]
    mem_specs = [tile(MEM_HD, mcol + hd) for hd in range(MEM_HEADS)] + [
        pl.BlockSpec((M, 2 * MW), lambda b, i: (b, 0))]
    assert (len(proj_specs), len(win_specs), len(mem_specs)) == (N_PROJ_IN, N_WIN_IN, N_MEM_IN)
    return pl.pallas_call(
        functools.partial(_side_kernel, scale, nsplit, nb, tq),
        grid=(B, nq),
        in_specs=proj_specs + win_specs + mem_specs,
        out_specs=[tile(N), tile(N), tile(N), tile(WW), tile(MW)],
        out_shape=[jax.ShapeDtypeStruct((T, N), BF16)] * 3 + [
            jax.ShapeDtypeStruct((T, WW), BF16), jax.ShapeDtypeStruct((T, MW), BF16)],
        compiler_params=_cparams(("parallel", "parallel")),
        name="proj_local_attn",
    )(h, w_all, g_q, g_kv, w_qm, w_kv, *q_tabs, *k_tabs,
      bias, sink.astype(F32), zw, zw, zw, zw, zw, zw, zw,
      *([zw] * MEM_HEADS), mem_kv)


def _merge_kernel(h_ref, a_ref, b_ref, c_ref, wga_ref, wgb_ref, wgc_ref,
                  wa_ref, wb_ref, wc_ref, o_ref):
    h = h_ref[...]

    def branch(x_ref, wg_ref, w_ref):
        gate = jax.nn.sigmoid(jnp.dot(h, wg_ref[...], preferred_element_type=F32))
        return gate * jnp.dot(x_ref[...], w_ref[...], preferred_element_type=F32)

    o_ref[...] = (branch(a_ref, wga_ref, wa_ref) + branch(b_ref, wgb_ref, wb_ref)
                  + branch(c_ref, wgc_ref, wc_ref)).astype(o_ref.dtype)


def gated_merge(h, a, b, c, w_g, w_a, w_b, w_c, l, tm=1024, tn=MERGE_TN):
    T, D = h.shape
    tm, tn = _tile(T, tm), _tile(D, tn)
    nj = D // tn
    assert GATE_OFF % tn == 0
    g0 = GATE_OFF // tn
    act = lambda w: pl.BlockSpec((tm, w), lambda i, j: (i, 0))
    wg = lambda br: pl.BlockSpec((None, D, tn), lambda i, j: (l, 0, g0 + br * nj + j))
    wx = lambda k: pl.BlockSpec((None, k, tn), lambda i, j: (l, 0, j))
    return pl.pallas_call(
        _merge_kernel,
        grid=(T // tm, nj),
        in_specs=[act(D), act(a.shape[1]), act(b.shape[1]), act(c.shape[1]),
                  wg(0), wg(1), wg(2),
                  wx(a.shape[1]), wx(b.shape[1]), wx(c.shape[1])],
        out_specs=pl.BlockSpec((tm, tn), lambda i, j: (i, j)),
        out_shape=jax.ShapeDtypeStruct((T, D), BF16),
        compiler_params=_cparams(("parallel", "parallel")),
        name="gated_merge",
    )(h, a, b, c, w_g, w_g, w_g, w_a, w_b, w_c)


def _outproj_kernel(nsplit, m_ref, w_ref, x_ref, gpost_ref, gnext_ref, xo_ref, ho_ref):
    rows = m_ref.shape[0] // nsplit
    for p in range(nsplit):
        r = slice(p * rows, (p + 1) * rows)
        y = jnp.dot(m_ref[r, :], w_ref[...], preferred_element_type=F32)
        x_new = x_ref[r, :] + _rms(y, gpost_ref[...])
        xo_ref[r, :] = x_new
        ho_ref[r, :] = _rms(x_new, gnext_ref[...]).astype(ho_ref.dtype)


def out_proj(merged, w_out, x, g_post, g_next, l, tm=512, nsplit=4):
    T, D = x.shape
    tm = _tile(T, tm)
    row = lambda: pl.BlockSpec((tm, D), lambda i: (i, 0))
    return pl.pallas_call(
        functools.partial(_outproj_kernel, nsplit),
        grid=(T // tm,),
        in_specs=[row(), pl.BlockSpec((None, D, D), lambda i: (l, 0, 0)), row(),
                  _vec_spec(D, l, 1), _vec_spec(D, l, 1)],
        out_specs=[row(), row()],
        out_shape=[jax.ShapeDtypeStruct((T, D), F32), jax.ShapeDtypeStruct((T, D), BF16)],
        compiler_params=_cparams(("parallel",)),
        name="out_proj",
    )(merged, w_out, x, g_post, g_next)


def _ffn_kernel(emit_h, nsplit, nf, h_ref, wu_ref, wd_ref, x_ref, gpost_ref, gnext_ref, xo_ref, *rest):
    ho_ref, acc_ref = rest if emit_h else (None, rest[0])
    k = pl.program_id(1)
    last = pl.num_programs(1) - 1

    def partial_sum(r):
        u = jnp.dot(h_ref[r, :], wu_ref[...], preferred_element_type=F32)
        u = jnp.square(jnp.maximum(u, 0.0)).astype(BF16)
        return jnp.dot(u, wd_ref[...], preferred_element_type=F32)

    if nf > 1:
        @pl.when(k == 0)
        def _():
            acc_ref[...] = partial_sum(slice(None))

        @pl.when((k > 0) & (k < last))
        def _():
            acc_ref[...] += partial_sum(slice(None))

    @pl.when(k == last)
    def _():
        rows = h_ref.shape[0] // nsplit
        for p in range(nsplit):
            r = slice(p * rows, (p + 1) * rows)
            y = partial_sum(r)
            if nf > 1:
                y = acc_ref[r, :] + y
            x_new = x_ref[r, :] + _rms(y, gpost_ref[...])
            xo_ref[r, :] = x_new
            if emit_h:
                ho_ref[r, :] = _rms(x_new, gnext_ref[...]).astype(ho_ref.dtype)


def ffn(h, w_up, w_down, x, g_post, g_next, l, l_next, tm=512, tf=FFN_TF, nsplit=2):
    T, D = x.shape
    Fd = w_up.shape[2]
    tm, tf = _tile(T, tm), _tile(Fd, tf)
    nf = Fd // tf
    emit_h = l_next is not None
    row = lambda: pl.BlockSpec((tm, D), lambda i, k: (i, 0))
    out_specs = [row()] + ([row()] if emit_h else [])
    out_shape = [jax.ShapeDtypeStruct((T, D), F32)] + (
        [jax.ShapeDtypeStruct((T, D), BF16)] if emit_h else [])
    res = pl.pallas_call(
        functools.partial(_ffn_kernel, emit_h, nsplit, nf),
        grid=(T // tm, nf),
        in_specs=[row(),
                  pl.BlockSpec((None, D, tf), lambda i, k: (l, 0, k)),
                  pl.BlockSpec((None, tf, D), lambda i, k: (l, k, 0)),
                  row(), _vec_spec(D, l, 2), _vec_spec(D, l_next if emit_h else l, 2)],
        out_specs=out_specs,
        out_shape=out_shape,
        scratch_shapes=[pltpu.VMEM((tm, D), F32)],
        compiler_params=_cparams(("parallel", "arbitrary")),
        name="ffn",
    )(h, w_up, w_down, x, g_post, g_next)
    return (res[0], res[1]) if emit_h else (res[0], None)


def _prep_in_proj(w_in):
    L, D, _ = w_in.shape
    o_lat = Q_LORA + KV_LORA + QK_ROPE
    o_qw = o_lat + WIN_HEADS * WIN_HD
    o_kw = o_qw + WIN_KV_HEADS * WIN_HD
    o_vw = o_kw + WIN_KV_HEADS * WIN_HD
    o_qm = o_vw + MEM_HEADS * MEM_HD
    assert o_qm - o_lat == QKV_W and LAT_W + QKV_W <= GATE_OFF
    z = lambda n: jnp.zeros((L, D, n), w_in.dtype)
    w_all = jnp.concatenate([w_in[:, :, :o_lat], z(LAT_W - o_lat), w_in[:, :, o_lat:o_qm],
                             z(GATE_OFF - LAT_W - QKV_W), w_in[:, :, o_qm:]], axis=2)
    return w_all.astype(BF16)


def _prep_mla(w_uq, w_ukv):
    L = w_uq.shape[0]
    wq = w_uq.astype(BF16).reshape(L, Q_LORA, MLA_HEADS, QK_NOPE + QK_ROPE)
    wq = jnp.pad(wq, ((0, 0), (0, 0), (0, 0), (0, QK_PAD - QK_NOPE - QK_ROPE)))
    wq = wq.reshape(L, Q_LORA, MLA_HEADS * QK_PAD)
    wkv = w_ukv.astype(BF16).reshape(L, KV_LORA, MLA_HEADS, QK_NOPE + V_DIM)
    wkv = jnp.concatenate([wkv[..., :QK_NOPE].reshape(L, KV_LORA, -1),
                           wkv[..., QK_NOPE:].reshape(L, KV_LORA, -1)], axis=2)
    return wq, wkv


def kernel(x_prompt, x_sample, mem_prompt, mem_sample, g_attn_pre, g_attn_post, w_in, g_q_lat, g_kv_lat, w_uq, w_ukv, g_mem, w_mem_kv, sink_logit, w_branch_mla, w_branch_win, w_branch_mem, w_out, g_ffn_pre, g_ffn_post, w_ff_up, w_ff_down):
    depth = w_in.shape[0]
    scale = (QK_NOPE + QK_ROPE) ** -0.5 * LOG2E

    w_all = _prep_in_proj(w_in)
    wq, wkv = _prep_mla(w_uq, w_ukv)
    wmem = w_mem_kv.astype(BF16)
    wa = w_branch_mla.astype(BF16)
    wb = w_branch_win.astype(BF16)
    wc = w_branch_mem.astype(BF16)
    wo = w_out.astype(BF16)
    wup = w_ff_up.astype(BF16)
    wdn = w_ff_down.astype(BF16)
    vec = lambda g: g.reshape(depth, 1, g.shape[1])
    g_pre, g_post, g_q, g_kv = vec(g_attn_pre), vec(g_attn_post), vec(g_q_lat), vec(g_kv_lat)
    g_m, g_fpre, g_fpost = vec(g_mem), vec(g_ffn_pre), vec(g_ffn_post)

    def run(x3, mem3):
        B, S, D = x3.shape
        x = x3.reshape(B * S, D)
        mem = mem3.reshape(B * mem3.shape[1], D)
        h = rmsnorm_bf16(x, g_pre, 0)
        for l in range(depth):
            zw = matmul(h, w_all, l, BF16, LAT_W, QKV_W, tn=QKV_W // 2, name="in_proj_qkv")
            mem_kv = norm_matmul(mem, g_m, wmem, l, BF16)
            q, k, v, b, c = projections_and_local_attention(
                h, zw, mem_kv, sink_logit[l], w_all, g_q, g_kv, wq, wkv, l, B, S, scale)
            a = mla_attention(q, k, v, B, S)
            merged = gated_merge(h, a, b, c, w_all, wa, wb, wc, l)
            x, h = out_proj(merged, wo, x, g_post, g_fpre, l)
            x, h = ffn(h, wup, wdn, x, g_fpost, g_pre, l, l + 1 if l + 1 < depth else None)
        return x.reshape(B, S, D)

    return (run(x_prompt, mem_prompt), run(x_sample, mem_sample))
```
